```python
import math
import jax
import jax.numpy as jnp
from jax import lax
import numpy as np

D_MODEL = 1024
BATCH = 2
SEQ = 16384
DEPTH = 4

N_A_LAYERS = DEPTH // 2
N_B_LAYERS = DEPTH - N_A_LAYERS

GDN_HEADS = 8
GDN_DK = 128
GDN_DV = 256
GDN_KEY_W = GDN_HEADS * GDN_DK
GDN_VAL_W = GDN_HEADS * GDN_DV
GDN_CONV_W = 2 * GDN_KEY_W + GDN_VAL_W
Z_OFF = GDN_CONV_W
A_OFF = Z_OFF + GDN_VAL_W
B_OFF = A_OFF + GDN_HEADS
GDN_IN_W = B_OFF + GDN_HEADS
CONV_K = 4
CHUNK = 64

SB_HEADS = 4
SB_DH = 256
SB_W = SB_HEADS * SB_DH
SB_BLOCK = 128

EPS = 1e-6

kernel_name = 'yoco_gdn_stickbreaking_trunk'


def rmsnorm(x, g):
    xf = x.astype(jnp.float32)
    y = xf * lax.rsqrt(jnp.mean(xf * xf, axis=-1, keepdims=True) + EPS)
    return (y * g.astype(jnp.float32)).astype(x.dtype)


def l2norm(x):
    return x * lax.rsqrt(jnp.sum(x * x, axis=-1, keepdims=True) + EPS)


def causal_depthwise_conv(x, w):
    k_size, ch = w.shape
    return lax.conv_general_dilated(
        x, w[:, None, :].astype(x.dtype), window_strides=(1,),
        padding=[(k_size - 1, 0)], dimension_numbers=('NWC', 'WIO', 'NWC'),
        feature_group_count=ch)


def to_chunks(t):
    b, s, h = t.shape[:3]
    t = t.reshape((b, s // CHUNK, CHUNK, h) + t.shape[3:])
    return jnp.moveaxis(t, 3, 1)


def gated_delta_rule(q, k, v, g, beta):
    b, s, h, dk = q.shape
    dv = v.shape[-1]
    q, k, v, g, beta = (to_chunks(t) for t in (q, k, v, g, beta))
    gc = jnp.cumsum(g, axis=-1)
    idx = jnp.arange(CHUNK)
    incl = idx[:, None] >= idx[None, :]
    strict = idx[:, None] > idx[None, :]
    diff = gc[..., :, None] - gc[..., None, :]
    decay = jnp.where(incl, jnp.exp(jnp.where(incl, diff, 0.0)), 0.0)
    kb = k * beta[..., None]
    a_mat = jnp.where(strict, jnp.einsum('bhnid,bhnjd->bhnij', kb, k) * decay, 0.0)
    eye = jnp.eye(CHUNK, dtype=a_mat.dtype)
    rhs = jnp.concatenate([v * beta[..., None], kb * jnp.exp(gc)[..., None]], axis=-1)
    sol = lax.linalg.triangular_solve(a_mat + eye, rhs, left_side=True, lower=True,
                                      unit_diagonal=True)
    w_val, k_cum = sol[..., :dv], sol[..., dv:]
    qk = jnp.einsum('bhnid,bhnjd->bhnij', q, k) * decay
    q_dec = q * jnp.exp(gc)[..., None]
    k_dec = k * jnp.exp(gc[..., -1:] - gc)[..., None]
    g_last = jnp.exp(gc[..., -1])

    def step(state, inp):
        qk_n, qd_n, kd_n, wv_n, kc_n, gl_n = inp
        v_new = wv_n - jnp.einsum('bhik,bhkv->bhiv', kc_n, state)
        o_n = (jnp.einsum('bhik,bhkv->bhiv', qd_n, state)
               + jnp.einsum('bhij,bhjv->bhiv', qk_n, v_new))
        state = state * gl_n[..., None, None] + jnp.einsum('bhik,bhiv->bhkv', kd_n, v_new)
        return state, o_n

    xs = tuple(jnp.moveaxis(t, 2, 0) for t in (qk, q_dec, k_dec, w_val, k_cum, g_last))
    state0 = jnp.zeros((b, h, dk, dv), jnp.float32)
    _, o = lax.scan(step, state0, xs)
    return jnp.transpose(o, (1, 0, 3, 2, 4)).reshape(b, s, h, dv)


def gdn_layer(x, norm_g, w_in, conv_w, a_log, dt_bias, out_g, w_out):
    b, s, _ = x.shape
    proj = rmsnorm(x, norm_g) @ w_in
    qkv = jax.nn.silu(causal_depthwise_conv(proj[..., :GDN_CONV_W], conv_w)).astype(jnp.float32)
    z = proj[..., Z_OFF:A_OFF].astype(jnp.float32)
    a_in = proj[..., A_OFF:B_OFF].astype(jnp.float32)
    b_in = proj[..., B_OFF:].astype(jnp.float32)
    q = l2norm(qkv[..., :GDN_KEY_W].reshape(b, s, GDN_HEADS, GDN_DK)) * (GDN_DK ** -0.5)
    k = l2norm(qkv[..., GDN_KEY_W:2 * GDN_KEY_W].reshape(b, s, GDN_HEADS, GDN_DK))
    v = qkv[..., 2 * GDN_KEY_W:].reshape(b, s, GDN_HEADS, GDN_DV)
    g = -jnp.exp(a_log.astype(jnp.float32)) * jax.nn.softplus(a_in + dt_bias.astype(jnp.float32))
    beta = jax.nn.sigmoid(b_in)
    o = gated_delta_rule(q, k, v, g, beta)
    o = rmsnorm(o, out_g) * jax.nn.silu(z.reshape(b, s, GDN_HEADS, GDN_DV))
    return x + o.reshape(b, s, GDN_VAL_W).astype(x.dtype) @ w_out


def shared_kv(x, kv_norm, w_kv):
    b, s, _ = x.shape
    kv = rmsnorm(x, kv_norm) @ w_kv
    k = kv[..., :SB_W].reshape(b, s, SB_HEADS, SB_DH).transpose(0, 2, 1, 3).astype(jnp.float32)
    v = kv[..., SB_W:].reshape(b, s, SB_HEADS, SB_DH).transpose(0, 2, 1, 3).astype(jnp.float32)
    return k, v


def stick_breaking_attention(q, k, v):
    b, h, s_len, _ = q.shape
    sub = jnp.arange(SB_BLOCK)
    upper = (sub[:, None] >= sub[None, :]).astype(jnp.float32)
    outs = []
    for i in range(s_len // SB_BLOCK):
        n_kb = i + 1
        kv_len = n_kb * SB_BLOCK
        qb = q[:, :, i * SB_BLOCK:kv_len]
        kb = k[:, :, :kv_len]
        vb = v[:, :, :kv_len]
        z = jnp.einsum('bhqd,bhkd->bhqk', qb, kb)
        t_idx = i * SB_BLOCK + sub[:, None]
        s_idx = jnp.arange(kv_len)[None, :]
        mask = s_idx < t_idx
        log1m = jnp.where(mask, -jax.nn.softplus(z), 0.0)
        log1m = log1m.reshape(b, h, SB_BLOCK, n_kb, SB_BLOCK)
        rc_in = jnp.einsum('bhqnj,js->bhqns', log1m, upper,
                           precision=lax.Precision.HIGHEST)
        totals = rc_in[..., 0]
        later = lax.cumsum(totals, axis=3, reverse=True) - totals
        rc = (rc_in + later[..., None]).reshape(b, h, SB_BLOCK, kv_len)
        attn = jnp.where(mask, jnp.exp(z + rc), 0.0)
        outs.append(jnp.einsum('bhqk,bhkd->bhqd', attn, vb))
    return jnp.concatenate(outs, axis=2)


def sb_layer(x, norm_g, w_in, w_out, k_sh, v_sh):
    b, s, _ = x.shape
    proj = rmsnorm(x, norm_g) @ w_in
    q = proj[..., :SB_W].reshape(b, s, SB_HEADS, SB_DH).transpose(0, 2, 1, 3).astype(jnp.float32)
    q = q * (SB_DH ** -0.5)
    z = proj[..., SB_W:]
    o = stick_breaking_attention(q, k_sh, v_sh)
    o = o.transpose(0, 2, 1, 3).reshape(b, s, SB_W).astype(x.dtype) * jax.nn.silu(z)
    return x + o @ w_out


def setup_inputs(seed: int = 0) -> dict:
    key = jax.random.key(seed)
    ks = jax.random.split(key, 14)
    f32 = jnp.float32
    nrm = jax.random.normal
    x = nrm(ks[0], (BATCH, SEQ, D_MODEL), f32)
    a_norm = 1.0 + 0.01 * nrm(ks[1], (N_A_LAYERS, D_MODEL), f32)
    a_w_in = nrm(ks[2], (N_A_LAYERS, D_MODEL, GDN_IN_W), f32) * D_MODEL ** -0.5
    a_conv = nrm(ks[3], (N_A_LAYERS, CONV_K, GDN_CONV_W), f32) * CONV_K ** -0.5
    a_A_log = jnp.log(jax.random.uniform(ks[4], (N_A_LAYERS, GDN_HEADS), f32, 1.0, 16.0))
    dt = jnp.exp(jax.random.uniform(ks[5], (N_A_LAYERS, GDN_HEADS), f32,
                                    math.log(1e-3), math.log(1e-1)))
    a_dt_bias = dt + jnp.log(-jnp.expm1(-dt))
    a_out_norm = 1.0 + 0.01 * nrm(ks[6], (N_A_LAYERS, GDN_DV), f32)
    a_w_out = nrm(ks[7], (N_A_LAYERS, GDN_VAL_W, D_MODEL), f32) * GDN_VAL_W ** -0.5
    kv_norm = 1.0 + 0.01 * nrm(ks[8], (D_MODEL,), f32)
    w_kv = nrm(ks[9], (D_MODEL, 2 * SB_W), f32) * D_MODEL ** -0.5
    b_norm = 1.0 + 0.01 * nrm(ks[10], (N_B_LAYERS, D_MODEL), f32)
    b_w_in = nrm(ks[11], (N_B_LAYERS, D_MODEL, 2 * SB_W), f32) * D_MODEL ** -0.5
    b_w_out = nrm(ks[12], (N_B_LAYERS, SB_W, D_MODEL), f32) * SB_W ** -0.5
    final_norm = 1.0 + 0.01 * nrm(ks[13], (D_MODEL,), f32)
    return {'x': x, 'a_norm': a_norm, 'a_w_in': a_w_in, 'a_conv': a_conv,
            'a_A_log': a_A_log, 'a_dt_bias': a_dt_bias, 'a_out_norm': a_out_norm,
            'a_w_out': a_w_out, 'kv_norm': kv_norm, 'w_kv': w_kv, 'b_norm': b_norm,
            'b_w_in': b_w_in, 'b_w_out': b_w_out, 'final_norm': final_norm}


def reference(x, a_norm, a_w_in, a_conv, a_A_log, a_dt_bias, a_out_norm, a_w_out,
              kv_norm, w_kv, b_norm, b_w_in, b_w_out, final_norm):
    k_sh = None
    v_sh = None
    for layer in range(DEPTH):
        if layer < N_A_LAYERS:
            x = gdn_layer(x, a_norm[layer], a_w_in[layer], a_conv[layer], a_A_log[layer],
                          a_dt_bias[layer], a_out_norm[layer], a_w_out[layer])
        else:
            if layer == N_A_LAYERS:
                k_sh, v_sh = shared_kv(x, kv_norm, w_kv)
            j = layer - N_A_LAYERS
            x = sb_layer(x, b_norm[j], b_w_in[j], b_w_out[j], k_sh, v_sh)
    return rmsnorm(x, final_norm)
```

```python
import functools

import jax
import jax.numpy as jnp
from jax import lax
from jax.experimental import pallas as pl
from jax.experimental.pallas import tpu as pltpu

F32 = jnp.float32
BF16 = jnp.bfloat16

EPS = 1e-6
LANES = 128

GDN_HEADS = 8
GDN_DK = 128
GDN_DV = 256
GDN_KEY_W = GDN_HEADS * GDN_DK
GDN_VAL_W = GDN_HEADS * GDN_DV
GDN_MAIN_W = 2 * GDN_KEY_W + 2 * GDN_VAL_W
CONV_K = 4
GDN_CHUNK = 256
CONV_TAIL = 8
INV_BASE = 8

SB_HEADS = 4
SB_DH = 256
SB_W = SB_HEADS * SB_DH
SB_BLOCK = 256
SB_SKIP = 110.0

PROJ_TM = 512
VMEM_LIMIT = 48 * 1024 * 1024


def _mm(a, b):
    return jnp.dot(a.astype(BF16), b.astype(BF16), preferred_element_type=F32)


def _mm_nt(a, b):
    return lax.dot_general(a.astype(BF16), b.astype(BF16), (((1,), (1,)), ((), ())),
                           preferred_element_type=F32)


def _split3(x):
    hi = x.astype(BF16)
    r = x - hi.astype(F32)
    mid = r.astype(BF16)
    lo = (r - mid.astype(F32)).astype(BF16)
    return hi, mid, lo


def _softplus(x):
    return jnp.maximum(x, 0.0) + jnp.log1p(jnp.exp(-jnp.abs(x)))


def _silu(x):
    return x * jax.nn.sigmoid(x)


def _norm_proj_kernel(x_ref, g_ref, w_ref, o_ref, xn_ref):
    @pl.when(pl.program_id(1) == 0)
    def _():
        x = x_ref[...]
        ms = jnp.mean(x * x, axis=-1, keepdims=True)
        xn_ref[...] = (x * lax.rsqrt(ms + EPS) * g_ref[...]).astype(BF16)

    o_ref[...] = jnp.dot(xn_ref[...], w_ref[...], preferred_element_type=F32).astype(o_ref.dtype)


def norm_proj(x, g, w, out_dtype, tn):
    t, d = x.shape
    n = w.shape[1]
    return pl.pallas_call(
        _norm_proj_kernel,
        grid=(t // PROJ_TM, n // tn),
        in_specs=[pl.BlockSpec((PROJ_TM, d), lambda i, j: (i, 0)),
                  pl.BlockSpec((1, d), lambda i, j: (0, 0)),
                  pl.BlockSpec((d, tn), lambda i, j: (0, j))],
        out_specs=pl.BlockSpec((PROJ_TM, tn), lambda i, j: (i, j)),
        out_shape=jax.ShapeDtypeStruct((t, n), out_dtype),
        scratch_shapes=[pltpu.VMEM((PROJ_TM, d), BF16)],
        compiler_params=pltpu.CompilerParams(dimension_semantics=("arbitrary", "arbitrary"),
                                             vmem_limit_bytes=VMEM_LIMIT),
        name="norm_proj",
    )(x, g.reshape(1, d), w)


def _norm_proj_hp_kernel(x_ref, g_ref, w_ref, o_ref):
    x = x_ref[...]
    ms = jnp.mean(x * x, axis=-1, keepdims=True)
    xn = x * lax.rsqrt(ms + EPS) * g_ref[...]
    xh = xn.astype(BF16)
    xl = (xn - xh.astype(F32)).astype(BF16)
    w = w_ref[...]
    wh = w.astype(BF16)
    wl = (w - wh.astype(F32)).astype(BF16)
    dot = functools.partial(jnp.dot, preferred_element_type=F32)
    o_ref[...] = dot(xh, wh) + (dot(xl, wh) + dot(xh, wl))


def norm_proj_hp(x, g, w):
    t, d = x.shape
    n = w.shape[1]
    return pl.pallas_call(
        _norm_proj_hp_kernel,
        grid=(t // PROJ_TM,),
        in_specs=[pl.BlockSpec((PROJ_TM, d), lambda i: (i, 0)),
                  pl.BlockSpec((1, d), lambda i: (0, 0)),
                  pl.BlockSpec((d, n), lambda i: (0, 0))],
        out_specs=pl.BlockSpec((PROJ_TM, n), lambda i: (i, 0)),
        out_shape=jax.ShapeDtypeStruct((t, n), F32),
        compiler_params=pltpu.CompilerParams(dimension_semantics=("arbitrary",),
                                             vmem_limit_bytes=VMEM_LIMIT),
        name="norm_proj_hp",
    )(x, g.reshape(1, d), w)


def _out_proj_kernel(a_ref, w_ref, r_ref, o_ref):
    o_ref[...] = r_ref[...] + jnp.dot(a_ref[...], w_ref[...], preferred_element_type=F32)


def _out_proj_norm_kernel(a_ref, w_ref, r_ref, g_ref, o_ref):
    y = r_ref[...] + jnp.dot(a_ref[...], w_ref[...], preferred_element_type=F32)
    ms = jnp.mean(y * y, axis=-1, keepdims=True)
    o_ref[...] = y * lax.rsqrt(ms + EPS) * g_ref[...]


def out_proj(a, w, resid, final_g=None):
    t, k = a.shape
    n = w.shape[1]
    in_specs = [pl.BlockSpec((PROJ_TM, k), lambda i: (i, 0)),
                pl.BlockSpec((k, n), lambda i: (0, 0)),
                pl.BlockSpec((PROJ_TM, n), lambda i: (i, 0))]
    args = [a, w, resid]
    body = _out_proj_kernel
    if final_g is not None:
        in_specs.append(pl.BlockSpec((1, n), lambda i: (0, 0)))
        args.append(final_g.reshape(1, n))
        body = _out_proj_norm_kernel
    return pl.pallas_call(
        body,
        grid=(t // PROJ_TM,),
        in_specs=in_specs,
        out_specs=pl.BlockSpec((PROJ_TM, n), lambda i: (i, 0)),
        out_shape=jax.ShapeDtypeStruct((t, n), F32),
        compiler_params=pltpu.CompilerParams(dimension_semantics=("arbitrary",),
                                             vmem_limit_bytes=VMEM_LIMIT),
        name="out_proj",
    )(*args)


def _gdn_gates_kernel(ab_ref, alog_ref, dtb_ref, bg_ref, gct_ref):
    ab = ab_ref[...]
    c = GDN_CHUNK
    g = -jnp.exp(alog_ref[...]) * _softplus(ab + dtb_ref[...])
    beta = jax.nn.sigmoid(ab)
    row = lax.broadcasted_iota(jnp.int32, (c, c), 0)
    col = lax.broadcasted_iota(jnp.int32, (c, c), 1)
    tri = jnp.where(row >= col, 1.0, 0.0).astype(BF16)
    dot = functools.partial(jnp.dot, preferred_element_type=F32)
    hi, mid, lo = _split3(g)
    gc = dot(tri, hi) + (dot(tri, mid) + dot(tri, lo))
    lane = lax.broadcasted_iota(jnp.int32, ab.shape, 1)
    bg_ref[...] = jnp.where(lane < GDN_HEADS, gc, beta)
    gct_ref[...] = gc.T[0:GDN_HEADS, :]


def gdn_gates(ab, a_log, dt_bias):
    t = ab.shape[0]
    nblk = t // GDN_CHUNK
    pad = lambda v: jnp.pad(v.astype(F32), (0, LANES - v.shape[0])).reshape(1, LANES)
    return pl.pallas_call(
        _gdn_gates_kernel,
        grid=(nblk,),
        in_specs=[pl.BlockSpec((GDN_CHUNK, LANES), lambda i: (i, 0)),
                  pl.BlockSpec((1, LANES), lambda i: (0, 0)),
                  pl.BlockSpec((1, LANES), lambda i: (0, 0))],
        out_specs=[pl.BlockSpec((GDN_CHUNK, LANES), lambda i: (i, 0)),
                   pl.BlockSpec((None, GDN_HEADS, GDN_CHUNK), lambda i: (i, 0, 0))],
        out_shape=[jax.ShapeDtypeStruct((t, LANES), F32),
                   jax.ShapeDtypeStruct((nblk, GDN_HEADS, GDN_CHUNK), F32)],
        compiler_params=pltpu.CompilerParams(dimension_semantics=("arbitrary",),
                                             vmem_limit_bytes=VMEM_LIMIT),
        name="gdn_gates",
    )(ab, pad(a_log), pad(dt_bias))


def _unit_lower_inverse_minus_eye(a, r):
    n = a.shape[0]
    d = jnp.where(r < INV_BASE, a, 0.0)
    d2 = _mm(d, d)
    d3 = _mm(d, d2)
    d4 = _mm(d2, d2)
    e = d2 - d - d3
    e = e + d4 + _mm(e, d4)
    b = INV_BASE
    while b < n:
        m = jnp.where((r >= b) & (r < 2 * b), a, 0.0)
        x = m + _mm(e, m)
        e = e - x - _mm(x, e)
        b *= 2
    return e


def _gdn_core_kernel(q_ref, k_ref, v_ref, z_ref, bg_ref, gct_ref, cwq_ref, cwk_ref, cwv_ref, og_ref,
                     o_ref, xe_ref, s_ref):
    h = pl.program_id(1)
    c = GDN_CHUNK
    dk, dv = GDN_DK, GDN_DV

    @pl.when(pl.program_id(2) == 0)
    def _():
        xe_ref[0:CONV_TAIL, :] = jnp.zeros((CONV_TAIL, 2 * dk + dv), F32)
        s_ref[...] = jnp.zeros_like(s_ref)

    xe_ref[CONV_TAIL:CONV_TAIL + c, 0:dk] = q_ref[...]
    xe_ref[CONV_TAIL:CONV_TAIL + c, dk:2 * dk] = k_ref[...]
    xe_ref[CONV_TAIL:CONV_TAIL + c, 2 * dk:] = v_ref[...]

    def conv_silu(lo, hi, w_ref):
        acc = None
        for i in range(CONV_K):
            start = CONV_TAIL - (CONV_K - 1) + i
            term = xe_ref[start:start + c, lo:hi] * w_ref[i:i + 1, :]
            acc = term if acc is None else acc + term
        return _silu(acc)

    qc = conv_silu(0, dk, cwq_ref)
    kc = conv_silu(dk, 2 * dk, cwk_ref)
    v = conv_silu(2 * dk, 2 * dk + dv, cwv_ref)
    xe_ref[0:CONV_TAIL, :] = xe_ref[c:c + CONV_TAIL, :]

    q = qc * lax.rsqrt(jnp.sum(qc * qc, axis=-1, keepdims=True) + EPS) * (dk ** -0.5)
    k = kc * lax.rsqrt(jnp.sum(kc * kc, axis=-1, keepdims=True) + EPS)

    bg = bg_ref[...]
    lane = lax.broadcasted_iota(jnp.int32, bg.shape, 1)
    gc_col = jnp.sum(jnp.where(lane == h, bg, 0.0), axis=1, keepdims=True)
    beta = jnp.sum(jnp.where(lane == h + GDN_HEADS, bg, 0.0), axis=1, keepdims=True)
    gc_row = gct_ref[pl.ds(h, 1), :]
    gc_last = gc_row[:, c - 1:c]

    row = lax.broadcasted_iota(jnp.int32, (c, c), 0)
    col = lax.broadcasted_iota(jnp.int32, (c, c), 1)
    incl = row >= col
    r = row ^ col
    decay = jnp.where(incl, jnp.exp(jnp.where(incl, gc_col - gc_row, 0.0)), 0.0)

    kb = k * beta
    a = jnp.where(row > col, _mm_nt(kb, k) * decay, 0.0)
    qk = _mm_nt(q, k) * decay
    egc = jnp.exp(gc_col)
    rhs_v = v * beta
    rhs_k = kb * egc

    e = _unit_lower_inverse_minus_eye(a, r)
    w_val = rhs_v + _mm(e, rhs_v)
    k_cum = rhs_k + _mm(e, rhs_k)

    q_dec = q * egc
    k_dec = k * jnp.exp(gc_last - gc_col)
    state = s_ref[...]
    p = _mm(jnp.concatenate([k_cum, q_dec], axis=0), state)
    v_new = w_val - p[0:c]
    o = p[c:2 * c] + _mm(qk, v_new)
    s_ref[...] = state * jnp.exp(gc_last) + _mm(k_dec.T, v_new)

    on = o * lax.rsqrt(jnp.mean(o * o, axis=-1, keepdims=True) + EPS) * og_ref[...]
    o_ref[...] = (on * _silu(z_ref[...])).astype(o_ref.dtype)


def gdn_core(proj, bg, gct, conv_w, out_g, batch):
    t = proj.shape[0]
    c = GDN_CHUNK
    nt = t // batch // c
    dk, dv, nh = GDN_DK, GDN_DV, GDN_HEADS
    rowblk = lambda b, h, i: b * nt + i
    return pl.pallas_call(
        _gdn_core_kernel,
        grid=(batch, nh, nt),
        in_specs=[pl.BlockSpec((c, dk), lambda b, h, i: (rowblk(b, h, i), h)),
                  pl.BlockSpec((c, dk), lambda b, h, i: (rowblk(b, h, i), nh + h)),
                  pl.BlockSpec((c, dv), lambda b, h, i: (rowblk(b, h, i), nh + h)),
                  pl.BlockSpec((c, dv), lambda b, h, i: (rowblk(b, h, i), 2 * nh + h)),
                  pl.BlockSpec((c, LANES), lambda b, h, i: (rowblk(b, h, i), 0)),
                  pl.BlockSpec((None, nh, c), lambda b, h, i: (rowblk(b, h, i), 0, 0)),
                  pl.BlockSpec((CONV_K, dk), lambda b, h, i: (0, h)),
                  pl.BlockSpec((CONV_K, dk), lambda b, h, i: (0, nh + h)),
                  pl.BlockSpec((CONV_K, dv), lambda b, h, i: (0, nh + h)),
                  pl.BlockSpec((1, dv), lambda b, h, i: (0, 0))],
        out_specs=pl.BlockSpec((c, dv), lambda b, h, i: (rowblk(b, h, i), h)),
        out_shape=jax.ShapeDtypeStruct((t, nh * dv), BF16),
        scratch_shapes=[pltpu.VMEM((CONV_TAIL + c, 2 * dk + dv), F32),
                        pltpu.VMEM((dk, dv), F32)],
        compiler_params=pltpu.CompilerParams(
            dimension_semantics=("arbitrary", "arbitrary", "arbitrary"),
            vmem_limit_bytes=VMEM_LIMIT),
        name="gdn_core",
    )(proj, proj, proj, proj, bg, gct, conv_w, conv_w, conv_w, out_g.reshape(1, dv))


def gdn_layer(x, batch, norm_g, w_in, conv_w, a_log, dt_bias, out_g, w_out):
    w_main = w_in[:, :GDN_MAIN_W].astype(BF16)
    w_ab = jnp.pad(w_in[:, GDN_MAIN_W:], ((0, 0), (0, LANES - 2 * GDN_HEADS)))
    proj = norm_proj(x, norm_g, w_main, F32, tn=1536)
    ab = norm_proj_hp(x, norm_g, w_ab)
    bg, gct = gdn_gates(ab, a_log, dt_bias)
    o = gdn_core(proj, bg, gct, conv_w, out_g, batch)
    return out_proj(o, w_out.astype(BF16), x)


def _sb_core_kernel(q_ref, k_ref, v_ref, z_ref, o_ref, acc_ref, later_ref):
    i = pl.program_id(2)
    blk = SB_BLOCK
    q = (q_ref[...] * (SB_DH ** -0.5)).astype(BF16)
    row = lax.broadcasted_iota(jnp.int32, (blk, blk), 0)
    col = lax.broadcasted_iota(jnp.int32, (blk, blk), 1)
    suffix = jnp.where(row >= col, 1.0, 0.0).astype(BF16)
    dot = functools.partial(jnp.dot, preferred_element_type=F32)

    def visit(j, diagonal):
        start = pl.multiple_of(j * blk, blk)
        kb = k_ref[pl.ds(start, blk), :]
        vb = v_ref[pl.ds(start, blk), :]
        z = _mm_nt(q, kb)
        log1m = -_softplus(z)
        if diagonal:
            valid = col < row
            log1m = jnp.where(valid, log1m, 0.0)
        hi, mid, lo = _split3(log1m)
        rc_in = dot(hi, suffix) + (dot(mid, suffix) + dot(lo, suffix))
        later = later_ref[...]
        attn = jnp.exp(z + (rc_in + later))
        if diagonal:
            attn = jnp.where(valid, attn, 0.0)
        acc_ref[...] += dot(attn.astype(BF16), vb)
        later_ref[...] = later + rc_in[:, 0:1]

    acc_ref[...] = jnp.zeros_like(acc_ref)
    later_ref[...] = jnp.zeros_like(later_ref)
    visit(i, True)

    def cond(carry):
        j, worst = carry
        return jnp.logical_and(j >= 0, worst > -SB_SKIP)

    def body(carry):
        j, _ = carry
        visit(j, False)
        return j - 1, jnp.max(later_ref[...])

    lax.while_loop(cond, body, (i - 1, jnp.max(later_ref[...])))
    o_ref[...] = (acc_ref[...] * _silu(z_ref[...])).astype(o_ref.dtype)


def sb_core(proj, kv, batch):
    t = proj.shape[0]
    s = t // batch
    blk = SB_BLOCK
    nq = s // blk
    nh, dh = SB_HEADS, SB_DH
    return pl.pallas_call(
        _sb_core_kernel,
        grid=(batch, nh, nq),
        in_specs=[pl.BlockSpec((blk, dh), lambda b, h, i: (b * nq + i, h)),
                  pl.BlockSpec((s, dh), lambda b, h, i: (b, h)),
                  pl.BlockSpec((s, dh), lambda b, h, i: (b, nh + h)),
                  pl.BlockSpec((blk, dh), lambda b, h, i: (b * nq + i, nh + h))],
        out_specs=pl.BlockSpec((blk, dh), lambda b, h, i: (b * nq + i, h)),
        out_shape=jax.ShapeDtypeStruct((t, nh * dh), BF16),
        scratch_shapes=[pltpu.VMEM((blk, dh), F32),
                        pltpu.VMEM((blk, 1), F32)],
        compiler_params=pltpu.CompilerParams(
            dimension_semantics=("arbitrary", "arbitrary", "arbitrary"),
            vmem_limit_bytes=VMEM_LIMIT),
        name="sb_core",
    )(proj, kv, kv, proj)


def sb_layer(x, batch, norm_g, w_in, w_out, kv, final_g):
    proj = norm_proj(x, norm_g, w_in.astype(BF16), F32, tn=2 * SB_W)
    o = sb_core(proj, kv, batch)
    return out_proj(o, w_out.astype(BF16), x, final_g)


def kernel(x, a_norm, a_w_in, a_conv, a_A_log, a_dt_bias, a_out_norm, a_w_out, kv_norm, w_kv, b_norm,
           b_w_in, b_w_out, final_norm):
    batch, seq, d = x.shape
    h = x.reshape(batch * seq, d)
    for l in range(a_w_in.shape[0]):
        h = gdn_layer(h, batch, a_norm[l], a_w_in[l], a_conv[l], a_A_log[l], a_dt_bias[l],
                      a_out_norm[l], a_w_out[l])
    kv = norm_proj(h, kv_norm, w_kv.astype(BF16), BF16, tn=2 * SB_W)
    n_b = b_w_in.shape[0]
    for j in range(n_b):
        h = sb_layer(h, batch, b_norm[j], b_w_in[j], b_w_out[j], kv,
                     final_norm if j == n_b - 1 else None)
    return h.reshape(batch, seq, d)
```

```python
import functools

import jax
import jax.numpy as jnp
from jax import lax
from jax.experimental import pallas as pl
from jax.experimental.pallas import tpu as pltpu

F32 = jnp.float32
BF16 = jnp.bfloat16

EPS = 1e-6
LANES = 128

GDN_HEADS = 8
GDN_DK = 128
GDN_DV = 256
GDN_KEY_W = GDN_HEADS * GDN_DK
GDN_VAL_W = GDN_HEADS * GDN_DV
GDN_MAIN_W = 2 * GDN_KEY_W + 2 * GDN_VAL_W
CONV_K = 4
GDN_BLOCK = 256
GDN_CHUNK = 128
GDN_HEADS_PER_STEP = 2
CONV_TAIL = 8
INV_BASE = 8

SB_HEADS = 4
SB_DH = 256
SB_W = SB_HEADS * SB_DH
SB_BLOCK = 256
SB_SKIP = 110.0

PROJ_TM = 512
VMEM_LIMIT = 48 * 1024 * 1024


def _mm(a, b):
    return jnp.dot(a.astype(BF16), b.astype(BF16), preferred_element_type=F32)


def _mm_nt(a, b):
    return lax.dot_general(a.astype(BF16), b.astype(BF16), (((1,), (1,)), ((), ())),
                           preferred_element_type=F32)


def _split3(x):
    hi = x.astype(BF16)
    r = x - hi.astype(F32)
    mid = r.astype(BF16)
    lo = (r - mid.astype(F32)).astype(BF16)
    return hi, mid, lo


def _softplus(x):
    return jnp.maximum(x, 0.0) + jnp.log1p(jnp.exp(-jnp.abs(x)))


def _silu(x):
    return x * jax.nn.sigmoid(x)


def _norm_proj_kernel(x_ref, g_ref, w_ref, o_ref, xn_ref):
    @pl.when(pl.program_id(1) == 0)
    def _():
        x = x_ref[...]
        ms = jnp.mean(x * x, axis=-1, keepdims=True)
        xn_ref[...] = (x * lax.rsqrt(ms + EPS) * g_ref[...]).astype(BF16)

    o_ref[...] = jnp.dot(xn_ref[...], w_ref[...], preferred_element_type=F32).astype(o_ref.dtype)


def norm_proj(x, g, w, out_dtype, tn, tm=PROJ_TM):
    t, d = x.shape
    n = w.shape[1]
    return pl.pallas_call(
        _norm_proj_kernel,
        grid=(t // tm, n // tn),
        in_specs=[pl.BlockSpec((tm, d), lambda i, j: (i, 0)),
                  pl.BlockSpec((1, d), lambda i, j: (0, 0)),
                  pl.BlockSpec((d, tn), lambda i, j: (0, j))],
        out_specs=pl.BlockSpec((tm, tn), lambda i, j: (i, j)),
        out_shape=jax.ShapeDtypeStruct((t, n), out_dtype),
        scratch_shapes=[pltpu.VMEM((tm, d), BF16)],
        compiler_params=pltpu.CompilerParams(dimension_semantics=("arbitrary", "arbitrary"),
                                             vmem_limit_bytes=VMEM_LIMIT),
        name="norm_proj",
    )(x, g.reshape(1, d), w)


def _norm_proj_hp_kernel(x_ref, g_ref, w_ref, o_ref):
    x = x_ref[...]
    ms = jnp.mean(x * x, axis=-1, keepdims=True)
    xn = x * lax.rsqrt(ms + EPS) * g_ref[...]
    xh = xn.astype(BF16)
    xl = (xn - xh.astype(F32)).astype(BF16)
    w = w_ref[...]
    wh = w.astype(BF16)
    wl = (w - wh.astype(F32)).astype(BF16)
    dot = functools.partial(jnp.dot, preferred_element_type=F32)
    o_ref[...] = dot(xh, wh) + (dot(xl, wh) + dot(xh, wl))


def norm_proj_hp(x, g, w):
    t, d = x.shape
    n = w.shape[1]
    return pl.pallas_call(
        _norm_proj_hp_kernel,
        grid=(t // PROJ_TM,),
        in_specs=[pl.BlockSpec((PROJ_TM, d), lambda i: (i, 0)),
                  pl.BlockSpec((1, d), lambda i: (0, 0)),
                  pl.BlockSpec((d, n), lambda i: (0, 0))],
        out_specs=pl.BlockSpec((PROJ_TM, n), lambda i: (i, 0)),
        out_shape=jax.ShapeDtypeStruct((t, n), F32),
        compiler_params=pltpu.CompilerParams(dimension_semantics=("arbitrary",),
                                             vmem_limit_bytes=VMEM_LIMIT),
        name="norm_proj_hp",
    )(x, g.reshape(1, d), w)


def _out_proj_kernel(a_ref, w_ref, r_ref, o_ref):
    o_ref[...] = r_ref[...] + jnp.dot(a_ref[...], w_ref[...], preferred_element_type=F32)


def _out_proj_norm_kernel(a_ref, w_ref, r_ref, g_ref, o_ref):
    y = r_ref[...] + jnp.dot(a_ref[...], w_ref[...], preferred_element_type=F32)
    ms = jnp.mean(y * y, axis=-1, keepdims=True)
    o_ref[...] = y * lax.rsqrt(ms + EPS) * g_ref[...]


def out_proj(a, w, resid, final_g=None):
    t, k = a.shape
    n = w.shape[1]
    in_specs = [pl.BlockSpec((PROJ_TM, k), lambda i: (i, 0)),
                pl.BlockSpec((k, n), lambda i: (0, 0)),
                pl.BlockSpec((PROJ_TM, n), lambda i: (i, 0))]
    args = [a, w, resid]
    body = _out_proj_kernel
    if final_g is not None:
        in_specs.append(pl.BlockSpec((1, n), lambda i: (0, 0)))
        args.append(final_g.reshape(1, n))
        body = _out_proj_norm_kernel
    return pl.pallas_call(
        body,
        grid=(t // PROJ_TM,),
        in_specs=in_specs,
        out_specs=pl.BlockSpec((PROJ_TM, n), lambda i: (i, 0)),
        out_shape=jax.ShapeDtypeStruct((t, n), F32),
        compiler_params=pltpu.CompilerParams(dimension_semantics=("arbitrary",),
                                             vmem_limit_bytes=VMEM_LIMIT),
        name="out_proj",
    )(*args)


def _gdn_gates_kernel(ab_ref, alog_ref, dtb_ref, bg_ref, gct_ref):
    ab = ab_ref[...]
    tb = GDN_BLOCK
    g = -jnp.exp(alog_ref[...]) * _softplus(ab + dtb_ref[...])
    beta = jax.nn.sigmoid(ab)
    row = lax.broadcasted_iota(jnp.int32, (tb, tb), 0)
    col = lax.broadcasted_iota(jnp.int32, (tb, tb), 1)
    same_chunk = (row ^ col) < GDN_CHUNK
    tri = jnp.where((row >= col) & same_chunk, 1.0, 0.0).astype(BF16)
    dot = functools.partial(jnp.dot, preferred_element_type=F32)
    hi, mid, lo = _split3(g)
    gc = dot(tri, hi) + (dot(tri, mid) + dot(tri, lo))
    lane = lax.broadcasted_iota(jnp.int32, ab.shape, 1)
    bg_ref[...] = jnp.where(lane < GDN_HEADS, gc, beta)
    gct_ref[...] = gc.T[0:GDN_HEADS, :]


def gdn_gates(ab, a_log, dt_bias):
    t = ab.shape[0]
    nblk = t // GDN_BLOCK
    pad = lambda v: jnp.pad(v.astype(F32), (0, LANES - v.shape[0])).reshape(1, LANES)
    return pl.pallas_call(
        _gdn_gates_kernel,
        grid=(nblk,),
        in_specs=[pl.BlockSpec((GDN_BLOCK, LANES), lambda i: (i, 0)),
                  pl.BlockSpec((1, LANES), lambda i: (0, 0)),
                  pl.BlockSpec((1, LANES), lambda i: (0, 0))],
        out_specs=[pl.BlockSpec((GDN_BLOCK, LANES), lambda i: (i, 0)),
                   pl.BlockSpec((None, GDN_HEADS, GDN_BLOCK), lambda i: (i, 0, 0))],
        out_shape=[jax.ShapeDtypeStruct((t, LANES), F32),
                   jax.ShapeDtypeStruct((nblk, GDN_HEADS, GDN_BLOCK), F32)],
        compiler_params=pltpu.CompilerParams(dimension_semantics=("arbitrary",),
                                             vmem_limit_bytes=VMEM_LIMIT),
        name="gdn_gates",
    )(ab, pad(a_log), pad(dt_bias))


def _unit_lower_inverse_minus_eye(mats, r):
    n = mats[0].shape[0]
    d = [jnp.where(r < INV_BASE, a, 0.0) for a in mats]
    d2 = [_mm(x, x) for x in d]
    d3 = [_mm(x, y) for x, y in zip(d, d2)]
    d4 = [_mm(y, y) for y in d2]
    e = [y - x - z for x, y, z in zip(d, d2, d3)]
    e = [x + y + _mm(x, y) for x, y in zip(e, d4)]
    b = INV_BASE
    while b < n:
        m = [jnp.where((r >= b) & (r < 2 * b), a, 0.0) for a in mats]
        x = [mi + _mm(ei, mi) for ei, mi in zip(e, m)]
        e = [ei - xi - _mm(xi, ei) for ei, xi in zip(e, x)]
        b *= 2
    return e


def _gdn_core_kernel(q_ref, k_ref, v_ref, z_ref, bg_ref, gct_ref, cwq_ref, cwk_ref, cwv_ref, og_ref,
                     o_ref, xe_ref, s_ref):
    hp = GDN_HEADS_PER_STEP
    h0 = pl.program_id(1) * hp
    tb, c = GDN_BLOCK, GDN_CHUNK
    dk, dv = GDN_DK, GDN_DV
    wk, wv = hp * dk, hp * dv

    @pl.when(pl.program_id(2) == 0)
    def _():
        xe_ref[0:CONV_TAIL, :] = jnp.zeros((CONV_TAIL, 2 * wk + wv), F32)
        s_ref[...] = jnp.zeros_like(s_ref)

    xe_ref[CONV_TAIL:CONV_TAIL + tb, 0:wk] = q_ref[...]
    xe_ref[CONV_TAIL:CONV_TAIL + tb, wk:2 * wk] = k_ref[...]
    xe_ref[CONV_TAIL:CONV_TAIL + tb, 2 * wk:] = v_ref[...]

    def conv_silu(lo, hi, w_ref):
        acc = None
        for i in reversed(range(CONV_K)):
            start = CONV_TAIL - (CONV_K - 1) + i
            term = xe_ref[start:start + tb, lo:hi] * w_ref[i:i + 1, :]
            acc = term if acc is None else acc + term
        return _silu(acc)

    qc = conv_silu(0, wk, cwq_ref)
    kc = conv_silu(wk, 2 * wk, cwk_ref)
    vc = conv_silu(2 * wk, 2 * wk + wv, cwv_ref)
    xe_ref[0:CONV_TAIL, :] = xe_ref[tb:tb + CONV_TAIL, :]

    bg = bg_ref[...]
    lane = lax.broadcasted_iota(jnp.int32, bg.shape, 1)
    row = lax.broadcasted_iota(jnp.int32, (c, c), 0)
    col = lax.broadcasted_iota(jnp.int32, (c, c), 1)
    incl = row >= col
    strict = row > col
    r = row ^ col

    units = [(hh, ci) for ci in range(tb // c) for hh in range(hp)]
    q, k, v, gc_col, beta, gc_row = [], [], [], [], [], []
    for hh in range(hp):
        qh = qc[:, hh * dk:(hh + 1) * dk]
        kh = kc[:, hh * dk:(hh + 1) * dk]
        q.append(qh * lax.rsqrt(jnp.sum(qh * qh, axis=-1, keepdims=True) + EPS) * (dk ** -0.5))
        k.append(kh * lax.rsqrt(jnp.sum(kh * kh, axis=-1, keepdims=True) + EPS))
        v.append(vc[:, hh * dv:(hh + 1) * dv])
        gc_col.append(jnp.sum(jnp.where(lane == h0 + hh, bg, 0.0), axis=1, keepdims=True))
        beta.append(jnp.sum(jnp.where(lane == h0 + hh + GDN_HEADS, bg, 0.0), axis=1, keepdims=True))
        gc_row.append(gct_ref[pl.ds(h0 + hh, 1), :])
    pick = lambda arrs: [arrs[hh][ci * c:(ci + 1) * c] for hh, ci in units]
    q, k, v, gc_col, beta = pick(q), pick(k), pick(v), pick(gc_col), pick(beta)
    gc_row = [gc_row[hh][:, ci * c:(ci + 1) * c] for hh, ci in units]
    gc_last = [g[:, c - 1:c] for g in gc_row]

    k_t = [x.T for x in k]
    decay = [jnp.where(incl, jnp.exp(jnp.where(incl, gcc - gcr, 0.0)), 0.0)
             for gcc, gcr in zip(gc_col, gc_row)]
    kb = [ki * bi for ki, bi in zip(k, beta)]
    a = [jnp.where(strict, _mm(kbi, kti) * di, 0.0) for kbi, kti, di in zip(kb, k_t, decay)]
    qk = [_mm(qi, kti) * di for qi, kti, di in zip(q, k_t, decay)]
    egc = [jnp.exp(g) for g in gc_col]
    rhs = [jnp.concatenate([kbi * ei, vi * bi], axis=1) for kbi, ei, vi, bi in zip(kb, egc, v, beta)]
    e = _unit_lower_inverse_minus_eye(a, r)
    sol = [x + _mm(ei, x) for ei, x in zip(e, rhs)]
    q_dec = [qi * ei for qi, ei in zip(q, egc)]
    k_dec_t = [kti * jnp.exp(gl - gcr) for kti, gl, gcr in zip(k_t, gc_last, gc_row)]
    both = [_mm(jnp.concatenate([kdt, qki], axis=0), x) for kdt, qki, x in zip(k_dec_t, qk, sol)]
    s_coef = [jnp.concatenate([x[0:dk, 0:dk], qd - x[dk:, 0:dk]], axis=0) for x, qd in zip(both, q_dec)]
    s_add = [x[0:dk, dk:] for x in both]
    o_add = [x[dk:, dk:] for x in both]

    state = [s_ref[hh] for hh in range(hp)]
    outs = [[] for _ in range(hp)]
    for u, (hh, ci) in enumerate(units):
        ps = _mm(s_coef[u], state[hh])
        outs[hh].append(o_add[u] + ps[dk:])
        state[hh] = state[hh] * jnp.exp(gc_last[u]) + (s_add[u] - ps[0:dk])
    for hh in range(hp):
        s_ref[hh] = state[hh]
        o = jnp.concatenate(outs[hh], axis=0)
        on = o * lax.rsqrt(jnp.mean(o * o, axis=-1, keepdims=True) + EPS) * og_ref[...]
        gate = _silu(z_ref[:, hh * dv:(hh + 1) * dv])
        o_ref[:, hh * dv:(hh + 1) * dv] = (on * gate).astype(o_ref.dtype)


def gdn_core(proj, bg, gct, conv_w, out_g, batch):
    t = proj.shape[0]
    tb = GDN_BLOCK
    nt = t // batch // tb
    hp = GDN_HEADS_PER_STEP
    dk, dv, nh = GDN_DK, GDN_DV, GDN_HEADS
    wk, wv, ng = hp * dk, hp * dv, nh // hp
    rowblk = lambda b, g, i: b * nt + i
    return pl.pallas_call(
        _gdn_core_kernel,
        grid=(batch, ng, nt),
        in_specs=[pl.BlockSpec((tb, wk), lambda b, g, i: (rowblk(b, g, i), g)),
                  pl.BlockSpec((tb, wk), lambda b, g, i: (rowblk(b, g, i), ng + g)),
                  pl.BlockSpec((tb, wv), lambda b, g, i: (rowblk(b, g, i), ng + g)),
                  pl.BlockSpec((tb, wv), lambda b, g, i: (rowblk(b, g, i), 2 * ng + g)),
                  pl.BlockSpec((tb, LANES), lambda b, g, i: (rowblk(b, g, i), 0)),
                  pl.BlockSpec((None, nh, tb), lambda b, g, i: (rowblk(b, g, i), 0, 0)),
                  pl.BlockSpec((CONV_K, wk), lambda b, g, i: (0, g)),
                  pl.BlockSpec((CONV_K, wk), lambda b, g, i: (0, ng + g)),
                  pl.BlockSpec((CONV_K, wv), lambda b, g, i: (0, ng + g)),
                  pl.BlockSpec((1, dv), lambda b, g, i: (0, 0))],
        out_specs=pl.BlockSpec((tb, wv), lambda b, g, i: (rowblk(b, g, i), g)),
        out_shape=jax.ShapeDtypeStruct((t, nh * dv), BF16),
        scratch_shapes=[pltpu.VMEM((CONV_TAIL + tb, 2 * wk + wv), F32),
                        pltpu.VMEM((hp, dk, dv), F32)],
        compiler_params=pltpu.CompilerParams(
            dimension_semantics=("arbitrary", "arbitrary", "arbitrary"),
            vmem_limit_bytes=VMEM_LIMIT),
        name="gdn_core",
    )(proj, proj, proj, proj, bg, gct, conv_w, conv_w, conv_w, out_g.reshape(1, dv))


def gdn_layer(x, batch, norm_g, w_in, conv_w, a_log, dt_bias, out_g, w_out):
    w_main = w_in[:, :GDN_MAIN_W].astype(BF16)
    w_ab = jnp.pad(w_in[:, GDN_MAIN_W:], ((0, 0), (0, LANES - 2 * GDN_HEADS)))
    proj = norm_proj(x, norm_g, w_main, F32, tn=1536, tm=2 * PROJ_TM)
    ab = norm_proj_hp(x, norm_g, w_ab)
    bg, gct = gdn_gates(ab, a_log, dt_bias)
    o = gdn_core(proj, bg, gct, conv_w, out_g, batch)
    return out_proj(o, w_out.astype(BF16), x)


def _sb_core_kernel(q_ref, k_ref, v_ref, z_ref, o_ref, acc_ref, later_ref):
    i = pl.program_id(2)
    blk = SB_BLOCK
    q = (q_ref[...] * (SB_DH ** -0.5)).astype(BF16)
    row = lax.broadcasted_iota(jnp.int32, (blk, blk), 0)
    col = lax.broadcasted_iota(jnp.int32, (blk, blk), 1)
    suffix = jnp.where(row >= col, 1.0, 0.0).astype(BF16)
    dot = functools.partial(jnp.dot, preferred_element_type=F32)

    def scores(j):
        start = pl.multiple_of(j * blk, blk)
        z = _mm_nt(q, k_ref[pl.ds(start, blk), :])
        return z, -_softplus(z)

    def suffix_sums(log1m):
        hi, mid, lo = _split3(log1m)
        return dot(hi, suffix) + (dot(mid, suffix) + dot(lo, suffix))

    def weighted_values(j, attn):
        start = pl.multiple_of(j * blk, blk)
        return dot(attn.astype(BF16), v_ref[pl.ds(start, blk), :])

    has_prev = i > 0
    jp = jnp.maximum(i - 1, 0)
    z_d, l_d = scores(i)
    z_p, l_p = scores(jp)
    valid = col < row
    l_d = jnp.where(valid, l_d, 0.0)
    l_p = jnp.where(has_prev, l_p, 0.0)
    rc_d = suffix_sums(l_d)
    rc_p = suffix_sums(l_p)
    later_d = rc_d[:, 0:1]
    attn_d = jnp.where(valid, jnp.exp(z_d + rc_d), 0.0)
    attn_p = jnp.where(has_prev, jnp.exp(z_p + (rc_p + later_d)), 0.0)
    acc_ref[...] = weighted_values(i, attn_d) + weighted_values(jp, attn_p)
    later_ref[...] = later_d + rc_p[:, 0:1]

    def cond(carry):
        j, worst = carry
        return jnp.logical_and(j >= 0, worst > -SB_SKIP)

    def body(carry):
        j, _ = carry
        z, log1m = scores(j)
        rc_in = suffix_sums(log1m)
        later = later_ref[...]
        acc_ref[...] += weighted_values(j, jnp.exp(z + (rc_in + later)))
        later_ref[...] = later + rc_in[:, 0:1]
        return j - 1, jnp.max(later_ref[...])

    lax.while_loop(cond, body, (i - 2, jnp.max(later_ref[...])))
    o_ref[...] = (acc_ref[...] * _silu(z_ref[...])).astype(o_ref.dtype)


def sb_core(proj, kv, batch):
    t = proj.shape[0]
    s = t // batch
    blk = SB_BLOCK
    nq = s // blk
    nh, dh = SB_HEADS, SB_DH
    return pl.pallas_call(
        _sb_core_kernel,
        grid=(batch, nh, nq),
        in_specs=[pl.BlockSpec((blk, dh), lambda b, h, i: (b * nq + i, h)),
                  pl.BlockSpec((s, dh), lambda b, h, i: (b, h)),
                  pl.BlockSpec((s, dh), lambda b, h, i: (b, nh + h)),
                  pl.BlockSpec((blk, dh), lambda b, h, i: (b * nq + i, nh + h))],
        out_specs=pl.BlockSpec((blk, dh), lambda b, h, i: (b * nq + i, h)),
        out_shape=jax.ShapeDtypeStruct((t, nh * dh), BF16),
        scratch_shapes=[pltpu.VMEM((blk, dh), F32),
                        pltpu.VMEM((blk, 1), F32)],
        compiler_params=pltpu.CompilerParams(
            dimension_semantics=("arbitrary", "arbitrary", "arbitrary"),
            vmem_limit_bytes=VMEM_LIMIT),
        name="sb_core",
    )(proj, kv, kv, proj)


def sb_layer(x, batch, norm_g, w_in, w_out, kv, final_g):
    proj = norm_proj(x, norm_g, w_in.astype(BF16), F32, tn=2 * SB_W)
    o = sb_core(proj, kv, batch)
    return out_proj(o, w_out.astype(BF16), x, final_g)


def kernel(x, a_norm, a_w_in, a_conv, a_A_log, a_dt_bias, a_out_norm, a_w_out, kv_norm, w_kv, b_norm,
           b_w_in, b_w_out, final_norm):
    batch, seq, d = x.shape
    h = x.reshape(batch * seq, d)
    for l in range(a_w_in.shape[0]):
        h = gdn_layer(h, batch, a_norm[l], a_w_in[l], a_conv[l], a_A_log[l], a_dt_bias[l],
                      a_out_norm[l], a_w_out[l])
    kv = norm_proj(h, kv_norm, w_kv.astype(BF16), BF16, tn=2 * SB_W)
    n_b = b_w_in.shape[0]
    for j in range(n_b):
        h = sb_layer(h, batch, b_norm[j], b_w_in[j], b_w_out[j], kv,
                     final_norm if j == n_b - 1 else None)
    return h.reshape(batch, seq, d)
```

```python
import functools

import jax
import jax.numpy as jnp
from jax import lax
from jax.experimental import pallas as pl
from jax.experimental.pallas import tpu as pltpu

F32 = jnp.float32
BF16 = jnp.bfloat16

EPS = 1e-6
LANES = 128

GDN_HEADS = 8
GDN_DK = 128
GDN_DV = 256
GDN_KEY_W = GDN_HEADS * GDN_DK
GDN_VAL_W = GDN_HEADS * GDN_DV
GDN_MAIN_W = 2 * GDN_KEY_W + 2 * GDN_VAL_W
CONV_K = 4
GDN_BLOCK = 256
GDN_CHUNK = 128
GDN_HEADS_PER_STEP = 8
CONV_TAIL = 8
GDN_PROJ_TM = 1024
GDN_PROJ_TN = 2048
GDN_PROJ_SUB = 128
INV_BASE = 8

SB_HEADS = 4
SB_DH = 256
SB_W = SB_HEADS * SB_DH
SB_BLOCK = 256
SB_QBLOCKS = 2
SB_SKIP = 110.0

PROJ_TM = 512
VMEM_LIMIT = 48 * 1024 * 1024


def _mm(a, b):
    return jnp.dot(a.astype(BF16), b.astype(BF16), preferred_element_type=F32)


def _mm_nt(a, b):
    return lax.dot_general(a.astype(BF16), b.astype(BF16), (((1,), (1,)), ((), ())),
                           preferred_element_type=F32)


def _split3(x):
    hi = x.astype(BF16)
    r = x - hi.astype(F32)
    mid = r.astype(BF16)
    lo = (r - mid.astype(F32)).astype(BF16)
    return hi, mid, lo


def _softplus(x):
    return jnp.maximum(x, 0.0) + jnp.log1p(jnp.exp(-jnp.abs(x)))


def _silu(x):
    return x * jax.nn.sigmoid(x)


def _norm_proj_kernel(x_ref, g_ref, w_ref, o_ref, xn_ref):
    @pl.when(pl.program_id(1) == 0)
    def _():
        x = x_ref[...]
        ms = jnp.mean(x * x, axis=-1, keepdims=True)
        xn_ref[...] = (x * lax.rsqrt(ms + EPS) * g_ref[...]).astype(BF16)

    o_ref[...] = jnp.dot(xn_ref[...], w_ref[...], preferred_element_type=F32).astype(o_ref.dtype)


def norm_proj(x, g, w, out_dtype, tn, tm=PROJ_TM):
    t, d = x.shape
    n = w.shape[1]
    return pl.pallas_call(
        _norm_proj_kernel,
        grid=(t // tm, n // tn),
        in_specs=[pl.BlockSpec((tm, d), lambda i, j: (i, 0)),
                  pl.BlockSpec((1, d), lambda i, j: (0, 0)),
                  pl.BlockSpec((d, tn), lambda i, j: (0, j))],
        out_specs=pl.BlockSpec((tm, tn), lambda i, j: (i, j)),
        out_shape=jax.ShapeDtypeStruct((t, n), out_dtype),
        scratch_shapes=[pltpu.VMEM((tm, d), BF16)],
        compiler_params=pltpu.CompilerParams(dimension_semantics=("arbitrary", "arbitrary"),
                                             vmem_limit_bytes=VMEM_LIMIT),
        name="norm_proj",
    )(x, g.reshape(1, d), w)


def _norm_proj_hp_kernel(x_ref, g_ref, w_ref, o_ref):
    x = x_ref[...]
    ms = jnp.mean(x * x, axis=-1, keepdims=True)
    xn = x * lax.rsqrt(ms + EPS) * g_ref[...]
    xh = xn.astype(BF16)
    xl = (xn - xh.astype(F32)).astype(BF16)
    w = w_ref[...]
    wh = w.astype(BF16)
    wl = (w - wh.astype(F32)).astype(BF16)
    dot = functools.partial(jnp.dot, preferred_element_type=F32)
    o_ref[...] = dot(xh, wh) + (dot(xl, wh) + dot(xh, wl))


def norm_proj_hp(x, g, w):
    t, d = x.shape
    n = w.shape[1]
    return pl.pallas_call(
        _norm_proj_hp_kernel,
        grid=(t // PROJ_TM,),
        in_specs=[pl.BlockSpec((PROJ_TM, d), lambda i: (i, 0)),
                  pl.BlockSpec((1, d), lambda i: (0, 0)),
                  pl.BlockSpec((d, n), lambda i: (0, 0))],
        out_specs=pl.BlockSpec((PROJ_TM, n), lambda i: (i, 0)),
        out_shape=jax.ShapeDtypeStruct((t, n), F32),
        compiler_params=pltpu.CompilerParams(dimension_semantics=("arbitrary",),
                                             vmem_limit_bytes=VMEM_LIMIT),
        name="norm_proj_hp",
    )(x, g.reshape(1, d), w)


def _out_proj_kernel(a_ref, w_ref, r_ref, o_ref):
    o_ref[...] = r_ref[...] + jnp.dot(a_ref[...], w_ref[...], preferred_element_type=F32)


def _out_proj_norm_kernel(a_ref, w_ref, r_ref, g_ref, o_ref):
    y = r_ref[...] + jnp.dot(a_ref[...], w_ref[...], preferred_element_type=F32)
    ms = jnp.mean(y * y, axis=-1, keepdims=True)
    o_ref[...] = y * lax.rsqrt(ms + EPS) * g_ref[...]


def out_proj(a, w, resid, final_g=None):
    t, k = a.shape
    n = w.shape[1]
    in_specs = [pl.BlockSpec((PROJ_TM, k), lambda i: (i, 0)),
                pl.BlockSpec((k, n), lambda i: (0, 0)),
                pl.BlockSpec((PROJ_TM, n), lambda i: (i, 0))]
    args = [a, w, resid]
    body = _out_proj_kernel
    if final_g is not None:
        in_specs.append(pl.BlockSpec((1, n), lambda i: (0, 0)))
        args.append(final_g.reshape(1, n))
        body = _out_proj_norm_kernel
    return pl.pallas_call(
        body,
        grid=(t // PROJ_TM,),
        in_specs=in_specs,
        out_specs=pl.BlockSpec((PROJ_TM, n), lambda i: (i, 0)),
        out_shape=jax.ShapeDtypeStruct((t, n), F32),
        compiler_params=pltpu.CompilerParams(dimension_semantics=("arbitrary",),
                                             vmem_limit_bytes=VMEM_LIMIT),
        name="out_proj",
    )(*args)


def _gdn_in_proj_kernel(x_ref, g_ref, w_ref, cw_ref, o_ref, xn_ref, pre_ref, hist_ref, *, tiles_per_seq):
    i = pl.program_id(0)
    j = pl.program_id(1)
    tm, tn = o_ref.shape
    rs = GDN_PROJ_SUB
    nsub = tm // rs
    dk = GDN_DK

    @pl.when(j == 0)
    def _():
        x = x_ref[...]
        ms = jnp.mean(x * x, axis=-1, keepdims=True)
        xn_ref[...] = (x * lax.rsqrt(ms + EPS) * g_ref[...]).astype(BF16)

    def project(r):
        return jnp.dot(xn_ref[r * rs:(r + 1) * rs, :], w_ref[...], preferred_element_type=F32)

    @pl.when(j == 2)
    def _():
        for r in range(nsub):
            o_ref[r * rs:(r + 1) * rs, :] = project(r)

    def conv_branch(col_tile, normalize):
        def stage(r, pre):
            if r == 0:
                first = (i % tiles_per_seq) == 0
                pre_ref[0, 0:CONV_TAIL, :] = jnp.where(first, 0.0, hist_ref[col_tile])
            else:
                pre_ref[r, 0:CONV_TAIL, :] = pre_ref[r - 1, rs:rs + CONV_TAIL, :]
            pre_ref[r, CONV_TAIL:, :] = pre

        def finish(r):
            acc = None
            for tap in reversed(range(CONV_K)):
                start = CONV_TAIL - (CONV_K - 1) + tap
                term = pre_ref[r, start:start + rs, :] * cw_ref[tap:tap + 1, :]
                acc = term if acc is None else acc + term
            y = _silu(acc)
            if normalize:
                groups = []
                for c in range(tn // dk):
                    yc = y[:, c * dk:(c + 1) * dk]
                    inv = lax.rsqrt(jnp.sum(yc * yc, axis=-1, keepdims=True) + EPS)
                    if c < GDN_HEADS:
                        inv = inv * (dk ** -0.5)
                    groups.append(yc * inv)
                y = jnp.concatenate(groups, axis=1)
            o_ref[r * rs:(r + 1) * rs, :] = y

        stage(0, project(0))
        for r in range(1, nsub):
            pre = project(r)
            finish(r - 1)
            stage(r, pre)
        finish(nsub - 1)
        hist_ref[col_tile] = pre_ref[nsub - 1, rs:rs + CONV_TAIL, :]

    @pl.when(j == 0)
    def _():
        conv_branch(0, True)

    @pl.when(j == 1)
    def _():
        conv_branch(1, False)


def gdn_in_proj(x, g, w, conv_w, seq):
    t, d = x.shape
    n = w.shape[1]
    tm, tn = GDN_PROJ_TM, GDN_PROJ_TN
    assert n == 3 * tn and conv_w.shape[1] == 2 * tn and seq % tm == 0
    rs = GDN_PROJ_SUB
    return pl.pallas_call(
        functools.partial(_gdn_in_proj_kernel, tiles_per_seq=seq // tm),
        grid=(t // tm, n // tn),
        in_specs=[pl.BlockSpec((tm, d), lambda i, j: (i, 0)),
                  pl.BlockSpec((1, d), lambda i, j: (0, 0)),
                  pl.BlockSpec((d, tn), lambda i, j: (0, j)),
                  pl.BlockSpec((CONV_K, tn), lambda i, j: (0, jnp.minimum(j, 1)))],
        out_specs=pl.BlockSpec((tm, tn), lambda i, j: (i, j)),
        out_shape=jax.ShapeDtypeStruct((t, n), F32),
        scratch_shapes=[pltpu.VMEM((tm, d), BF16),
                        pltpu.VMEM((tm // rs, rs + CONV_TAIL, tn), F32),
                        pltpu.VMEM((2, CONV_TAIL, tn), F32)],
        compiler_params=pltpu.CompilerParams(dimension_semantics=("arbitrary", "arbitrary"),
                                             vmem_limit_bytes=VMEM_LIMIT),
        name="gdn_in_proj",
    )(x, g.reshape(1, d), w, conv_w)


def _gdn_gates_kernel(ab_ref, alog_ref, dtb_ref, bg_ref, gct_ref):
    ab = ab_ref[...]
    tb = GDN_BLOCK
    g = -jnp.exp(alog_ref[...]) * _softplus(ab + dtb_ref[...])
    beta = jax.nn.sigmoid(ab)
    row = lax.broadcasted_iota(jnp.int32, (tb, tb), 0)
    col = lax.broadcasted_iota(jnp.int32, (tb, tb), 1)
    same_chunk = (row ^ col) < GDN_CHUNK
    tri = jnp.where((row >= col) & same_chunk, 1.0, 0.0).astype(BF16)
    dot = functools.partial(jnp.dot, preferred_element_type=F32)
    hi, mid, lo = _split3(g)
    gc = dot(tri, hi) + (dot(tri, mid) + dot(tri, lo))
    lane = lax.broadcasted_iota(jnp.int32, ab.shape, 1)
    bg_ref[...] = jnp.where(lane < GDN_HEADS, gc, beta)
    gct_ref[...] = gc.T[0:GDN_HEADS, :]


def gdn_gates(ab, a_log, dt_bias):
    t = ab.shape[0]
    nblk = t // GDN_BLOCK
    pad = lambda v: jnp.pad(v.astype(F32), (0, LANES - v.shape[0])).reshape(1, LANES)
    return pl.pallas_call(
        _gdn_gates_kernel,
        grid=(nblk,),
        in_specs=[pl.BlockSpec((GDN_BLOCK, LANES), lambda i: (i, 0)),
                  pl.BlockSpec((1, LANES), lambda i: (0, 0)),
                  pl.BlockSpec((1, LANES), lambda i: (0, 0))],
        out_specs=[pl.BlockSpec((GDN_BLOCK, LANES), lambda i: (i, 0)),
                   pl.BlockSpec((None, GDN_HEADS, GDN_BLOCK), lambda i: (i, 0, 0))],
        out_shape=[jax.ShapeDtypeStruct((t, LANES), F32),
                   jax.ShapeDtypeStruct((nblk, GDN_HEADS, GDN_BLOCK), F32)],
        compiler_params=pltpu.CompilerParams(dimension_semantics=("arbitrary",),
                                             vmem_limit_bytes=VMEM_LIMIT),
        name="gdn_gates",
    )(ab, pad(a_log), pad(dt_bias))


def _unit_lower_inverse_minus_eye(mats, r):
    n = mats[0].shape[0]
    d = [jnp.where(r < INV_BASE, a, 0.0) for a in mats]
    d2 = [_mm(x, x) for x in d]
    d3 = [_mm(x, y) for x, y in zip(d, d2)]
    d4 = [_mm(y, y) for y in d2]
    e = [y - x - z for x, y, z in zip(d, d2, d3)]
    e = [x + y + _mm(x, y) for x, y in zip(e, d4)]
    b = INV_BASE
    while b < n:
        m = [jnp.where((r >= b) & (r < 2 * b), a, 0.0) for a in mats]
        x = [mi + _mm(ei, mi) for ei, mi in zip(e, m)]
        e = [ei - xi - _mm(xi, ei) for ei, xi in zip(e, x)]
        b *= 2
    return e


def _gdn_core_kernel(q_ref, k_ref, v_ref, z_ref, bg_ref, gct_ref, og_ref, o_ref, s_ref):
    hp = GDN_HEADS_PER_STEP
    h0 = pl.program_id(1) * hp
    tb, c = GDN_BLOCK, GDN_CHUNK
    dk, dv = GDN_DK, GDN_DV

    @pl.when(pl.program_id(2) == 0)
    def _():
        s_ref[...] = jnp.zeros_like(s_ref)

    bg = bg_ref[...]
    lane = lax.broadcasted_iota(jnp.int32, bg.shape, 1)
    row = lax.broadcasted_iota(jnp.int32, (c, c), 0)
    col = lax.broadcasted_iota(jnp.int32, (c, c), 1)
    incl = row >= col
    strict = row > col
    r = row ^ col

    units = [(hh, ci) for ci in range(tb // c) for hh in range(hp)]
    q, k, v, gc_col, beta, gc_row = [], [], [], [], [], []
    for hh in range(hp):
        q.append(q_ref[:, hh * dk:(hh + 1) * dk])
        k.append(k_ref[:, hh * dk:(hh + 1) * dk])
        v.append(v_ref[:, hh * dv:(hh + 1) * dv])
        gc_col.append(jnp.sum(jnp.where(lane == h0 + hh, bg, 0.0), axis=1, keepdims=True))
        beta.append(jnp.sum(jnp.where(lane == h0 + hh + GDN_HEADS, bg, 0.0), axis=1, keepdims=True))
        gc_row.append(gct_ref[pl.ds(h0 + hh, 1), :])
    pick = lambda arrs: [arrs[hh][ci * c:(ci + 1) * c] for hh, ci in units]
    q, k, v, gc_col, beta = pick(q), pick(k), pick(v), pick(gc_col), pick(beta)
    gc_row = [gc_row[hh][:, ci * c:(ci + 1) * c] for hh, ci in units]
    gc_last = [g[:, c - 1:c] for g in gc_row]

    k_t = [x.T for x in k]
    decay = [jnp.where(incl, jnp.exp(jnp.where(incl, gcc - gcr, 0.0)), 0.0)
             for gcc, gcr in zip(gc_col, gc_row)]
    kb = [ki * bi for ki, bi in zip(k, beta)]
    a = [jnp.where(strict, _mm(kbi, kti) * di, 0.0) for kbi, kti, di in zip(kb, k_t, decay)]
    qk = [_mm(qi, kti) * di for qi, kti, di in zip(q, k_t, decay)]
    egc = [jnp.exp(g) for g in gc_col]
    rhs = [jnp.concatenate([kbi * ei, vi * bi], axis=1) for kbi, ei, vi, bi in zip(kb, egc, v, beta)]
    e = _unit_lower_inverse_minus_eye(a, r)
    sol = [x + _mm(ei, x) for ei, x in zip(e, rhs)]
    q_dec = [qi * ei for qi, ei in zip(q, egc)]
    k_dec_t = [kti * jnp.exp(gl - gcr) for kti, gl, gcr in zip(k_t, gc_last, gc_row)]
    both = [_mm(jnp.concatenate([kdt, qki], axis=0), x) for kdt, qki, x in zip(k_dec_t, qk, sol)]
    s_coef = [jnp.concatenate([x[0:dk, 0:dk], qd - x[dk:, 0:dk]], axis=0) for x, qd in zip(both, q_dec)]
    s_add = [x[0:dk, dk:] for x in both]
    o_add = [x[dk:, dk:] for x in both]

    state = [s_ref[hh] for hh in range(hp)]
    outs = [[] for _ in range(hp)]
    for u, (hh, ci) in enumerate(units):
        ps = _mm(s_coef[u], state[hh])
        outs[hh].append(o_add[u] + ps[dk:])
        state[hh] = state[hh] * jnp.exp(gc_last[u]) + (s_add[u] - ps[0:dk])
    for hh in range(hp):
        s_ref[hh] = state[hh]
        o = jnp.concatenate(outs[hh], axis=0)
        on = o * lax.rsqrt(jnp.mean(o * o, axis=-1, keepdims=True) + EPS) * og_ref[...]
        gate = _silu(z_ref[:, hh * dv:(hh + 1) * dv])
        o_ref[:, hh * dv:(hh + 1) * dv] = (on * gate).astype(o_ref.dtype)


def gdn_core(proj, bg, gct, out_g, batch):
    t = proj.shape[0]
    tb = GDN_BLOCK
    nt = t // batch // tb
    hp = GDN_HEADS_PER_STEP
    dk, dv, nh = GDN_DK, GDN_DV, GDN_HEADS
    wk, wv, ng = hp * dk, hp * dv, nh // hp
    rowblk = lambda b, g, i: b * nt + i
    return pl.pallas_call(
        _gdn_core_kernel,
        grid=(batch, ng, nt),
        in_specs=[pl.BlockSpec((tb, wk), lambda b, g, i: (rowblk(b, g, i), g)),
                  pl.BlockSpec((tb, wk), lambda b, g, i: (rowblk(b, g, i), ng + g)),
                  pl.BlockSpec((tb, wv), lambda b, g, i: (rowblk(b, g, i), ng + g)),
                  pl.BlockSpec((tb, wv), lambda b, g, i: (rowblk(b, g, i), 2 * ng + g)),
                  pl.BlockSpec((tb, LANES), lambda b, g, i: (rowblk(b, g, i), 0)),
                  pl.BlockSpec((None, nh, tb), lambda b, g, i: (rowblk(b, g, i), 0, 0)),
                  pl.BlockSpec((1, dv), lambda b, g, i: (0, 0))],
        out_specs=pl.BlockSpec((tb, wv), lambda b, g, i: (rowblk(b, g, i), g)),
        out_shape=jax.ShapeDtypeStruct((t, nh * dv), BF16),
        scratch_shapes=[pltpu.VMEM((hp, dk, dv), F32)],
        compiler_params=pltpu.CompilerParams(
            dimension_semantics=("arbitrary", "arbitrary", "arbitrary"),
            vmem_limit_bytes=VMEM_LIMIT),
        name="gdn_core",
    )(proj, proj, proj, proj, bg, gct, out_g.reshape(1, dv))


def gdn_layer(x, batch, norm_g, w_in, conv_w, a_log, dt_bias, out_g, w_out):
    w_main = w_in[:, :GDN_MAIN_W].astype(BF16)
    w_ab = jnp.pad(w_in[:, GDN_MAIN_W:], ((0, 0), (0, LANES - 2 * GDN_HEADS)))
    proj = gdn_in_proj(x, norm_g, w_main, conv_w, x.shape[0] // batch)
    ab = norm_proj_hp(x, norm_g, w_ab)
    bg, gct = gdn_gates(ab, a_log, dt_bias)
    o = gdn_core(proj, bg, gct, out_g, batch)
    return out_proj(o, w_out.astype(BF16), x)


def _sb_core_kernel(q_ref, k_ref, v_ref, z_ref, o_ref, acc_ref, later_ref):
    blk, nqb = SB_BLOCK, SB_QBLOCKS
    i0 = pl.program_id(2) * nqb
    row = lax.broadcasted_iota(jnp.int32, (blk, blk), 0)
    col = lax.broadcasted_iota(jnp.int32, (blk, blk), 1)
    suffix = jnp.where(row >= col, 1.0, 0.0).astype(BF16)
    dot = functools.partial(jnp.dot, preferred_element_type=F32)
    rows = [slice(c * blk, (c + 1) * blk) for c in range(nqb)]
    q = [(q_ref[r, :] * (SB_DH ** -0.5)).astype(BF16) for r in rows]

    def scores(qc, j):
        start = pl.multiple_of(j * blk, blk)
        z = _mm_nt(qc, k_ref[pl.ds(start, blk), :])
        return z, -(jnp.maximum(z, 0.0) + jnp.log(1.0 + jnp.exp(-jnp.abs(z))))

    def suffix_sums(log1m):
        hi, mid, lo = _split3(log1m)
        return dot(hi, suffix) + (dot(mid, suffix) + dot(lo, suffix))

    def weighted_values(j, attn):
        start = pl.multiple_of(j * blk, blk)
        return dot(attn.astype(BF16), v_ref[pl.ds(start, blk), :])

    idx = [i0 + c for c in range(nqb)]
    has_prev = [ix > 0 for ix in idx]
    jp = [jnp.maximum(ix - 1, 0) for ix in idx]
    valid = col < row
    zl_d = [scores(qc, ix) for qc, ix in zip(q, idx)]
    zl_p = [scores(qc, j) for qc, j in zip(q, jp)]
    rc_d = [suffix_sums(jnp.where(valid, l, 0.0)) for _, l in zl_d]
    rc_p = [suffix_sums(jnp.where(hp, l, 0.0)) for (_, l), hp in zip(zl_p, has_prev)]
    later_d = [x[:, 0:1] for x in rc_d]
    attn_d = [jnp.where(valid, jnp.exp(z + rc), 0.0) for (z, _), rc in zip(zl_d, rc_d)]
    attn_p = [jnp.where(hp, jnp.exp(z + (rc + ld)), 0.0)
              for (z, _), rc, ld, hp in zip(zl_p, rc_p, later_d, has_prev)]
    for c, r in enumerate(rows):
        acc_ref[r, :] = weighted_values(idx[c], attn_d[c]) + weighted_values(jp[c], attn_p[c])
        later_ref[r, :] = later_d[c] + rc_p[c][:, 0:1]

    for c, r in enumerate(rows):
        def cond(carry):
            j, worst = carry
            return jnp.logical_and(j >= 0, worst > -SB_SKIP)

        def body(carry, c=c, r=r):
            j, _ = carry
            z, log1m = scores(q[c], j)
            rc_in = suffix_sums(log1m)
            later = later_ref[r, :]
            acc_ref[r, :] += weighted_values(j, jnp.exp(z + (rc_in + later)))
            later_ref[r, :] = later + rc_in[:, 0:1]
            return j - 1, jnp.max(later_ref[r, :])

        lax.while_loop(cond, body, (idx[c] - 2, jnp.max(later_ref[r, :])))
    o_ref[...] = (acc_ref[...] * _silu(z_ref[...])).astype(o_ref.dtype)


def sb_core(proj, kv, batch):
    t = proj.shape[0]
    s = t // batch
    qrows = SB_BLOCK * SB_QBLOCKS
    nq = s // qrows
    nh, dh = SB_HEADS, SB_DH
    return pl.pallas_call(
        _sb_core_kernel,
        grid=(batch, nh, nq),
        in_specs=[pl.BlockSpec((qrows, dh), lambda b, h, i: (b * nq + i, h)),
                  pl.BlockSpec((s, dh), lambda b, h, i: (b, h)),
                  pl.BlockSpec((s, dh), lambda b, h, i: (b, nh + h)),
                  pl.BlockSpec((qrows, dh), lambda b, h, i: (b * nq + i, nh + h))],
        out_specs=pl.BlockSpec((qrows, dh), lambda b, h, i: (b * nq + i, h)),
        out_shape=jax.ShapeDtypeStruct((t, nh * dh), BF16),
        scratch_shapes=[pltpu.VMEM((qrows, dh), F32),
                        pltpu.VMEM((qrows, 1), F32)],
        compiler_params=pltpu.CompilerParams(
            dimension_semantics=("arbitrary", "arbitrary", "arbitrary"),
            vmem_limit_bytes=VMEM_LIMIT),
        name="sb_core",
    )(proj, kv, kv, proj)


def sb_layer(x, batch, norm_g, w_in, w_out, kv, final_g):
    proj = norm_proj(x, norm_g, w_in.astype(BF16), F32, tn=2 * SB_W)
    o = sb_core(proj, kv, batch)
    return out_proj(o, w_out.astype(BF16), x, final_g)


def kernel(x, a_norm, a_w_in, a_conv, a_A_log, a_dt_bias, a_out_norm, a_w_out, kv_norm, w_kv, b_norm,
           b_w_in, b_w_out, final_norm):
    batch, seq, d = x.shape
    h = x.reshape(batch * seq, d)
    for l in range(a_w_in.shape[0]):
        h = gdn_layer(h, batch, a_norm[l], a_w_in[l], a_conv[l], a_A_log[l], a_dt_bias[l],
                      a_out_norm[l], a_w_out[l])
    kv = norm_proj(h, kv_norm, w_kv.astype(BF16), BF16, tn=2 * SB_W)
    n_b = b_w_in.shape[0]
    for j in range(n_b):
        h = sb_layer(h, batch, b_norm[j], b_w_in[j], b_w_out[j], kv,
                     final_norm if j == n_b - 1 else None)
    return h.reshape(batch, seq, d)
```

```python
import functools

import jax
import jax.numpy as jnp
from jax import lax
from jax.experimental import pallas as pl
from jax.experimental.pallas import tpu as pltpu

F32 = jnp.float32
BF16 = jnp.bfloat16

EPS = 1e-6
LANES = 128

GDN_HEADS = 8
GDN_DK = 128
GDN_DV = 256
GDN_KEY_W = GDN_HEADS * GDN_DK
GDN_VAL_W = GDN_HEADS * GDN_DV
GDN_MAIN_W = 2 * GDN_KEY_W + 2 * GDN_VAL_W
CONV_K = 4
GDN_BLOCK = 256
GDN_CHUNK = 128
GDN_HEADS_PER_STEP = 8
CONV_TAIL = 8
GDN_PROJ_TM = 1024
GDN_PROJ_TN = 2048
GDN_PROJ_SUB = 256
INV_BASE = 8

SB_HEADS = 4
SB_DH = 256
SB_W = SB_HEADS * SB_DH
SB_BLOCK = 256
SB_QBLOCKS = 4
SB_SKIP = 110.0

PROJ_TM = 512
VMEM_LIMIT = 48 * 1024 * 1024


def _mm(a, b):
    return jnp.dot(a.astype(BF16), b.astype(BF16), preferred_element_type=F32)


def _mm_nt(a, b):
    return lax.dot_general(a.astype(BF16), b.astype(BF16), (((1,), (1,)), ((), ())),
                           preferred_element_type=F32)


def _split3(x):
    hi = x.astype(BF16)
    r = x - hi.astype(F32)
    mid = r.astype(BF16)
    lo = (r - mid.astype(F32)).astype(BF16)
    return hi, mid, lo


def _softplus(x):
    return jnp.maximum(x, 0.0) + jnp.log1p(jnp.exp(-jnp.abs(x)))


def _silu(x):
    return x * jax.nn.sigmoid(x)


def _norm_proj_kernel(x_ref, g_ref, w_ref, o_ref, xn_ref):
    @pl.when(pl.program_id(1) == 0)
    def _():
        x = x_ref[...]
        ms = jnp.mean(x * x, axis=-1, keepdims=True)
        xn_ref[...] = (x * lax.rsqrt(ms + EPS) * g_ref[...]).astype(BF16)

    o_ref[...] = jnp.dot(xn_ref[...], w_ref[...], preferred_element_type=F32).astype(o_ref.dtype)


def norm_proj(x, g, w, out_dtype, tn, tm=PROJ_TM):
    t, d = x.shape
    n = w.shape[1]
    return pl.pallas_call(
        _norm_proj_kernel,
        grid=(t // tm, n // tn),
        in_specs=[pl.BlockSpec((tm, d), lambda i, j: (i, 0)),
                  pl.BlockSpec((1, d), lambda i, j: (0, 0)),
                  pl.BlockSpec((d, tn), lambda i, j: (0, j))],
        out_specs=pl.BlockSpec((tm, tn), lambda i, j: (i, j)),
        out_shape=jax.ShapeDtypeStruct((t, n), out_dtype),
        scratch_shapes=[pltpu.VMEM((tm, d), BF16)],
        compiler_params=pltpu.CompilerParams(dimension_semantics=("arbitrary", "arbitrary"),
                                             vmem_limit_bytes=VMEM_LIMIT),
        name="norm_proj",
    )(x, g.reshape(1, d), w)


def _out_proj_kernel(a_ref, w_ref, r_ref, o_ref):
    o_ref[...] = r_ref[...] + jnp.dot(a_ref[...], w_ref[...], preferred_element_type=F32)


def _out_proj_norm_kernel(a_ref, w_ref, r_ref, g_ref, o_ref):
    y = r_ref[...] + jnp.dot(a_ref[...], w_ref[...], preferred_element_type=F32)
    ms = jnp.mean(y * y, axis=-1, keepdims=True)
    o_ref[...] = y * lax.rsqrt(ms + EPS) * g_ref[...]


def out_proj(a, w, resid, final_g=None):
    t, k = a.shape
    n = w.shape[1]
    in_specs = [pl.BlockSpec((PROJ_TM, k), lambda i: (i, 0)),
                pl.BlockSpec((k, n), lambda i: (0, 0)),
                pl.BlockSpec((PROJ_TM, n), lambda i: (i, 0))]
    args = [a, w, resid]
    body = _out_proj_kernel
    if final_g is not None:
        in_specs.append(pl.BlockSpec((1, n), lambda i: (0, 0)))
        args.append(final_g.reshape(1, n))
        body = _out_proj_norm_kernel
    return pl.pallas_call(
        body,
        grid=(t // PROJ_TM,),
        in_specs=in_specs,
        out_specs=pl.BlockSpec((PROJ_TM, n), lambda i: (i, 0)),
        out_shape=jax.ShapeDtypeStruct((t, n), F32),
        compiler_params=pltpu.CompilerParams(dimension_semantics=("arbitrary",),
                                             vmem_limit_bytes=VMEM_LIMIT),
        name="out_proj",
    )(*args)


def _gdn_in_proj_kernel(x_ref, g_ref, w_ref, cw_ref, wab_ref, alog_ref, dtb_ref,
                        o_ref, bg_ref, gct_ref, xn_ref, pre_ref, hist_ref, *, tiles_per_seq):
    i = pl.program_id(0)
    j = pl.program_id(1)
    tm, tn = o_ref.shape
    rs = GDN_PROJ_SUB
    nsub = tm // rs
    dk = GDN_DK
    dot = functools.partial(jnp.dot, preferred_element_type=F32)

    @pl.when(j == 0)
    def _():
        x = x_ref[...]
        ms = jnp.mean(x * x, axis=-1, keepdims=True)
        xn = x * lax.rsqrt(ms + EPS) * g_ref[...]
        xh = xn.astype(BF16)
        xn_ref[...] = xh
        xl = (xn - xh.astype(F32)).astype(BF16)
        w = wab_ref[...]
        wh = w.astype(BF16)
        wl = (w - wh.astype(F32)).astype(BF16)
        ab = dot(xh, wh) + (dot(xl, wh) + dot(xh, wl))
        g = -jnp.exp(alog_ref[...]) * _softplus(ab + dtb_ref[...])
        beta = jax.nn.sigmoid(ab)
        tb = GDN_BLOCK
        row = lax.broadcasted_iota(jnp.int32, (tb, tb), 0)
        col = lax.broadcasted_iota(jnp.int32, (tb, tb), 1)
        tri = jnp.where((row >= col) & ((row ^ col) < GDN_CHUNK), 1.0, 0.0).astype(BF16)
        lane = lax.broadcasted_iota(jnp.int32, (tb, LANES), 1)
        for blk in range(tm // tb):
            rows = slice(blk * tb, (blk + 1) * tb)
            hi, mid, lo = _split3(g[rows])
            gc = dot(tri, hi) + (dot(tri, mid) + dot(tri, lo))
            bg_ref[rows, :] = jnp.where(lane < GDN_HEADS, gc, beta[rows])
            gct_ref[blk] = gc.T[0:GDN_HEADS, :]

    def project(r):
        return dot(xn_ref[r * rs:(r + 1) * rs, :], w_ref[...])

    @pl.when(j == 2)
    def _():
        for r in range(nsub):
            o_ref[r * rs:(r + 1) * rs, :] = _silu(project(r))

    def conv_branch(col_tile, normalize):
        def stage(r, pre):
            if r == 0:
                first = (i % tiles_per_seq) == 0
                pre_ref[0, 0:CONV_TAIL, :] = jnp.where(first, 0.0, hist_ref[col_tile])
            else:
                pre_ref[r, 0:CONV_TAIL, :] = pre_ref[r - 1, rs:rs + CONV_TAIL, :]
            pre_ref[r, CONV_TAIL:, :] = pre

        def finish(r):
            xe = pre_ref[r]
            acc = xe[CONV_TAIL:, :] * cw_ref[CONV_K - 1:CONV_K, :]
            for tap in reversed(range(CONV_K - 1)):
                shifted = pltpu.roll(xe, CONV_K - 1 - tap, axis=0)
                acc = acc + shifted[CONV_TAIL:, :] * cw_ref[tap:tap + 1, :]
            y = _silu(acc)
            if normalize:
                groups = []
                for c in range(tn // dk):
                    yc = y[:, c * dk:(c + 1) * dk]
                    inv = lax.rsqrt(jnp.sum(yc * yc, axis=-1, keepdims=True) + EPS)
                    if c < GDN_HEADS:
                        inv = inv * (dk ** -0.5)
                    groups.append(yc * inv)
                y = jnp.concatenate(groups, axis=1)
            o_ref[r * rs:(r + 1) * rs, :] = y

        stage(0, project(0))
        for r in range(1, nsub):
            pre = project(r)
            finish(r - 1)
            stage(r, pre)
        finish(nsub - 1)
        hist_ref[col_tile] = pre_ref[nsub - 1, rs:rs + CONV_TAIL, :]

    @pl.when(j == 0)
    def _():
        conv_branch(0, True)

    @pl.when(j == 1)
    def _():
        conv_branch(1, False)


def gdn_in_proj(x, g, w, conv_w, w_ab, a_log, dt_bias, seq):
    t, d = x.shape
    n = w.shape[1]
    tm, tn = GDN_PROJ_TM, GDN_PROJ_TN
    assert n == 3 * tn and conv_w.shape[1] == 2 * tn and seq % tm == 0 and tm % GDN_BLOCK == 0
    rs = GDN_PROJ_SUB
    nblk = tm // GDN_BLOCK
    lane_row = lambda v: jnp.pad(v.astype(F32), (0, LANES - v.shape[0])).reshape(1, LANES)
    return pl.pallas_call(
        functools.partial(_gdn_in_proj_kernel, tiles_per_seq=seq // tm),
        grid=(t // tm, n // tn),
        in_specs=[pl.BlockSpec((tm, d), lambda i, j: (i, 0)),
                  pl.BlockSpec((1, d), lambda i, j: (0, 0)),
                  pl.BlockSpec((d, tn), lambda i, j: (0, j)),
                  pl.BlockSpec((CONV_K, tn), lambda i, j: (0, jnp.minimum(j, 1))),
                  pl.BlockSpec((d, LANES), lambda i, j: (0, 0)),
                  pl.BlockSpec((1, LANES), lambda i, j: (0, 0)),
                  pl.BlockSpec((1, LANES), lambda i, j: (0, 0))],
        out_specs=[pl.BlockSpec((tm, tn), lambda i, j: (i, j)),
                   pl.BlockSpec((tm, LANES), lambda i, j: (i, 0)),
                   pl.BlockSpec((nblk, GDN_HEADS, GDN_BLOCK), lambda i, j: (i, 0, 0))],
        out_shape=[jax.ShapeDtypeStruct((t, n), F32),
                   jax.ShapeDtypeStruct((t, LANES), F32),
                   jax.ShapeDtypeStruct((t // GDN_BLOCK, GDN_HEADS, GDN_BLOCK), F32)],
        scratch_shapes=[pltpu.VMEM((tm, d), BF16),
                        pltpu.VMEM((tm // rs, rs + CONV_TAIL, tn), F32),
                        pltpu.VMEM((2, CONV_TAIL, tn), F32)],
        compiler_params=pltpu.CompilerParams(dimension_semantics=("arbitrary", "arbitrary"),
                                             vmem_limit_bytes=VMEM_LIMIT),
        name="gdn_in_proj",
    )(x, g.reshape(1, d), w, conv_w, w_ab, lane_row(a_log), lane_row(dt_bias))


def _unit_lower_inverse_minus_eye(mats, r):
    n = mats[0].shape[0]
    d = [jnp.where(r < INV_BASE, a, 0.0) for a in mats]
    d2 = [_mm(x, x) for x in d]
    d3 = [_mm(x, y) for x, y in zip(d, d2)]
    d4 = [_mm(y, y) for y in d2]
    e = [y - x - z for x, y, z in zip(d, d2, d3)]
    e = [x + y + _mm(x, y) for x, y in zip(e, d4)]
    b = INV_BASE
    while b < n:
        m = [jnp.where((r >= b) & (r < 2 * b), a, 0.0) for a in mats]
        x = [mi + _mm(ei, mi) for ei, mi in zip(e, m)]
        e = [ei - xi - _mm(xi, ei) for ei, xi in zip(e, x)]
        b *= 2
    return e


def _gdn_core_kernel(q_ref, k_ref, v_ref, gate_ref, bg_ref, gct_ref, og_ref, o_ref, s_ref):
    hp = GDN_HEADS_PER_STEP
    h0 = pl.program_id(1) * hp
    tb, c = GDN_BLOCK, GDN_CHUNK
    dk, dv = GDN_DK, GDN_DV

    @pl.when(pl.program_id(2) == 0)
    def _():
        s_ref[...] = jnp.zeros_like(s_ref)

    bg = bg_ref[...]
    lane = lax.broadcasted_iota(jnp.int32, bg.shape, 1)
    row = lax.broadcasted_iota(jnp.int32, (c, c), 0)
    col = lax.broadcasted_iota(jnp.int32, (c, c), 1)
    incl = row >= col
    strict = row > col
    r = row ^ col

    units = [(hh, ci) for ci in range(tb // c) for hh in range(hp)]
    q, k, v, gc_col, beta, gc_row = [], [], [], [], [], []
    for hh in range(hp):
        q.append(q_ref[:, hh * dk:(hh + 1) * dk])
        k.append(k_ref[:, hh * dk:(hh + 1) * dk])
        v.append(v_ref[:, hh * dv:(hh + 1) * dv])
        gc_col.append(jnp.sum(jnp.where(lane == h0 + hh, bg, 0.0), axis=1, keepdims=True))
        beta.append(jnp.sum(jnp.where(lane == h0 + hh + GDN_HEADS, bg, 0.0), axis=1, keepdims=True))
        gc_row.append(gct_ref[pl.ds(h0 + hh, 1), :])
    pick = lambda arrs: [arrs[hh][ci * c:(ci + 1) * c] for hh, ci in units]
    q, k, v, gc_col, beta = pick(q), pick(k), pick(v), pick(gc_col), pick(beta)
    gc_row = [gc_row[hh][:, ci * c:(ci + 1) * c] for hh, ci in units]
    gc_last = [g[:, c - 1:c] for g in gc_row]

    k_t = [x.T for x in k]
    decay = [jnp.where(incl, jnp.exp(jnp.where(incl, gcc - gcr, 0.0)), 0.0)
             for gcc, gcr in zip(gc_col, gc_row)]
    kb = [ki * bi for ki, bi in zip(k, beta)]
    a = [jnp.where(strict, _mm(kbi, kti) * di, 0.0) for kbi, kti, di in zip(kb, k_t, decay)]
    qk = [_mm(qi, kti) * di for qi, kti, di in zip(q, k_t, decay)]
    egc = [jnp.exp(g) for g in gc_col]
    rhs = [jnp.concatenate([kbi * ei, vi * bi], axis=1) for kbi, ei, vi, bi in zip(kb, egc, v, beta)]
    e = _unit_lower_inverse_minus_eye(a, r)
    sol = [x + _mm(ei, x) for ei, x in zip(e, rhs)]
    q_dec = [qi * ei for qi, ei in zip(q, egc)]
    k_dec_t = [kti * jnp.exp(gl - gcr) for kti, gl, gcr in zip(k_t, gc_last, gc_row)]
    both = [_mm(jnp.concatenate([kdt, qki], axis=0), x) for kdt, qki, x in zip(k_dec_t, qk, sol)]
    s_coef = [jnp.concatenate([x[0:dk, 0:dk], qd - x[dk:, 0:dk]], axis=0) for x, qd in zip(both, q_dec)]
    s_add = [x[0:dk, dk:] for x in both]
    o_add = [x[dk:, dk:] for x in both]

    state = [s_ref[hh] for hh in range(hp)]
    outs = [[] for _ in range(hp)]
    for u, (hh, ci) in enumerate(units):
        ps = _mm(s_coef[u], state[hh])
        outs[hh].append(o_add[u] + ps[dk:])
        state[hh] = state[hh] * jnp.exp(gc_last[u]) + (s_add[u] - ps[0:dk])
    for hh in range(hp):
        s_ref[hh] = state[hh]
        o = jnp.concatenate(outs[hh], axis=0)
        on = o * lax.rsqrt(jnp.mean(o * o, axis=-1, keepdims=True) + EPS) * og_ref[...]
        o_ref[:, hh * dv:(hh + 1) * dv] = (on * gate_ref[:, hh * dv:(hh + 1) * dv]).astype(o_ref.dtype)


def gdn_core(proj, bg, gct, out_g, batch):
    t = proj.shape[0]
    tb = GDN_BLOCK
    nt = t // batch // tb
    hp = GDN_HEADS_PER_STEP
    dk, dv, nh = GDN_DK, GDN_DV, GDN_HEADS
    wk, wv, ng = hp * dk, hp * dv, nh // hp
    rowblk = lambda b, g, i: b * nt + i
    return pl.pallas_call(
        _gdn_core_kernel,
        grid=(batch, ng, nt),
        in_specs=[pl.BlockSpec((tb, wk), lambda b, g, i: (rowblk(b, g, i), g)),
                  pl.BlockSpec((tb, wk), lambda b, g, i: (rowblk(b, g, i), ng + g)),
                  pl.BlockSpec((tb, wv), lambda b, g, i: (rowblk(b, g, i), ng + g)),
                  pl.BlockSpec((tb, wv), lambda b, g, i: (rowblk(b, g, i), 2 * ng + g)),
                  pl.BlockSpec((tb, LANES), lambda b, g, i: (rowblk(b, g, i), 0)),
                  pl.BlockSpec((None, nh, tb), lambda b, g, i: (rowblk(b, g, i), 0, 0)),
                  pl.BlockSpec((1, dv), lambda b, g, i: (0, 0))],
        out_specs=pl.BlockSpec((tb, wv), lambda b, g, i: (rowblk(b, g, i), g)),
        out_shape=jax.ShapeDtypeStruct((t, nh * dv), BF16),
        scratch_shapes=[pltpu.VMEM((hp, dk, dv), F32)],
        compiler_params=pltpu.CompilerParams(
            dimension_semantics=("arbitrary", "arbitrary", "arbitrary"),
            vmem_limit_bytes=VMEM_LIMIT),
        name="gdn_core",
    )(proj, proj, proj, proj, bg, gct, out_g.reshape(1, dv))


def gdn_layer(x, batch, norm_g, w_in, conv_w, a_log, dt_bias, out_g, w_out):
    w_main = w_in[:, :GDN_MAIN_W].astype(BF16)
    w_ab = jnp.pad(w_in[:, GDN_MAIN_W:], ((0, 0), (0, LANES - 2 * GDN_HEADS)))
    proj, bg, gct = gdn_in_proj(x, norm_g, w_main, conv_w, w_ab, a_log, dt_bias, x.shape[0] // batch)
    o = gdn_core(proj, bg, gct, out_g, batch)
    return out_proj(o, w_out.astype(BF16), x)


def _sb_core_kernel(q_ref, k_ref, v_ref, z_ref, o_ref, acc_ref, later_ref):
    blk, nqb = SB_BLOCK, SB_QBLOCKS
    i0 = pl.program_id(2) * nqb
    row = lax.broadcasted_iota(jnp.int32, (blk, blk), 0)
    col = lax.broadcasted_iota(jnp.int32, (blk, blk), 1)
    suffix = jnp.where(row >= col, 1.0, 0.0).astype(BF16)
    dot = functools.partial(jnp.dot, preferred_element_type=F32)
    rows = [slice(c * blk, (c + 1) * blk) for c in range(nqb)]
    q = [(q_ref[r, :] * (SB_DH ** -0.5)).astype(BF16) for r in rows]

    def scores(qc, j):
        start = pl.multiple_of(j * blk, blk)
        z = _mm_nt(qc, k_ref[pl.ds(start, blk), :])
        return z, -(jnp.maximum(z, 0.0) + jnp.log(1.0 + jnp.exp(-jnp.abs(z))))

    def suffix_sums(log1m):
        hi, mid, lo = _split3(log1m)
        return dot(hi, suffix) + (dot(mid, suffix) + dot(lo, suffix))

    def weighted_values(j, attn):
        start = pl.multiple_of(j * blk, blk)
        return dot(attn.astype(BF16), v_ref[pl.ds(start, blk), :])

    idx = [i0 + c for c in range(nqb)]
    has_prev = [ix > 0 for ix in idx]
    jp = [jnp.maximum(ix - 1, 0) for ix in idx]
    valid = col < row
    zl_d = [scores(qc, ix) for qc, ix in zip(q, idx)]
    zl_p = [scores(qc, j) for qc, j in zip(q, jp)]
    rc_d = [suffix_sums(jnp.where(valid, l, 0.0)) for _, l in zl_d]
    rc_p = [suffix_sums(jnp.where(hp, l, 0.0)) for (_, l), hp in zip(zl_p, has_prev)]
    later_d = [x[:, 0:1] for x in rc_d]
    attn_d = [jnp.where(valid, jnp.exp(z + rc), 0.0) for (z, _), rc in zip(zl_d, rc_d)]
    attn_p = [jnp.where(hp, jnp.exp(z + (rc + ld)), 0.0)
              for (z, _), rc, ld, hp in zip(zl_p, rc_p, later_d, has_prev)]
    for c, r in enumerate(rows):
        acc_ref[r, :] = weighted_values(idx[c], attn_d[c]) + weighted_values(jp[c], attn_p[c])
        later_ref[r, :] = later_d[c] + rc_p[c][:, 0:1]

    for c, r in enumerate(rows):
        def cond(carry):
            j, worst = carry
            return jnp.logical_and(j >= 0, worst > -SB_SKIP)

        def body(carry, c=c, r=r):
            j, _ = carry
            z, log1m = scores(q[c], j)
            rc_in = suffix_sums(log1m)
            later = later_ref[r, :]
            acc_ref[r, :] += weighted_values(j, jnp.exp(z + (rc_in + later)))
            later_ref[r, :] = later + rc_in[:, 0:1]
            return j - 1, jnp.max(later_ref[r, :])

        lax.while_loop(cond, body, (idx[c] - 2, jnp.max(later_ref[r, :])))
    o_ref[...] = (acc_ref[...] * _silu(z_ref[...])).astype(o_ref.dtype)


def sb_core(proj, kv, batch):
    t = proj.shape[0]
    s = t // batch
    qrows = SB_BLOCK * SB_QBLOCKS
    nq = s // qrows
    nh, dh = SB_HEADS, SB_DH
    return pl.pallas_call(
        _sb_core_kernel,
        grid=(batch, nh, nq),
        in_specs=[pl.BlockSpec((qrows, dh), lambda b, h, i: (b * nq + i, h)),
                  pl.BlockSpec((s, dh), lambda b, h, i: (b, h)),
                  pl.BlockSpec((s, dh), lambda b, h, i: (b, nh + h)),
                  pl.BlockSpec((qrows, dh), lambda b, h, i: (b * nq + i, nh + h))],
        out_specs=pl.BlockSpec((qrows, dh), lambda b, h, i: (b * nq + i, h)),
        out_shape=jax.ShapeDtypeStruct((t, nh * dh), BF16),
        scratch_shapes=[pltpu.VMEM((qrows, dh), F32),
                        pltpu.VMEM((qrows, 1), F32)],
        compiler_params=pltpu.CompilerParams(
            dimension_semantics=("arbitrary", "arbitrary", "arbitrary"),
            vmem_limit_bytes=VMEM_LIMIT),
        name="sb_core",
    )(proj, kv, kv, proj)


def sb_layer(x, batch, norm_g, w_in, w_out, kv, final_g):
    proj = norm_proj(x, norm_g, w_in.astype(BF16), F32, tn=2 * SB_W)
    o = sb_core(proj, kv, batch)
    return out_proj(o, w_out.astype(BF16), x, final_g)


def kernel(x, a_norm, a_w_in, a_conv, a_A_log, a_dt_bias, a_out_norm, a_w_out, kv_norm, w_kv, b_norm,
           b_w_in, b_w_out, final_norm):
    batch, seq, d = x.shape
    h = x.reshape(batch * seq, d)
    for l in range(a_w_in.shape[0]):
        h = gdn_layer(h, batch, a_norm[l], a_w_in[l], a_conv[l], a_A_log[l], a_dt_bias[l],
                      a_out_norm[l], a_w_out[l])
    kv = norm_proj(h, kv_norm, w_kv.astype(BF16), BF16, tn=2 * SB_W)
    n_b = b_w_in.shape[0]
    for j in range(n_b):
        h = sb_layer(h, batch, b_norm[j], b_w_in[j], b_w_out[j], kv,
                     final_norm if j == n_b - 1 else None)
    return h.reshape(batch, seq, d)
```

```python
import functools

import jax
import jax.numpy as jnp
from jax import lax
from jax.experimental import pallas as pl
from jax.experimental.pallas import tpu as pltpu

F32 = jnp.float32
BF16 = jnp.bfloat16

EPS = 1e-6
LANES = 128

GDN_HEADS = 8
GDN_DK = 128
GDN_DV = 256
GDN_KEY_W = GDN_HEADS * GDN_DK
GDN_VAL_W = GDN_HEADS * GDN_DV
GDN_MAIN_W = 2 * GDN_KEY_W + 2 * GDN_VAL_W
CONV_K = 4
GDN_BLOCK = 256
GDN_CHUNK = 128
CONV_TAIL = 8
GDN_PROJ_TM = 1024
GDN_PROJ_TN = 2048
GDN_PROJ_SUB = 256
INV_BASE = 8

SB_HEADS = 4
SB_DH = 256
SB_W = SB_HEADS * SB_DH
SB_BLOCK = 256
SB_QBLOCKS = 4
SB_SKIP = 110.0

PROJ_TM = 512
VMEM_LIMIT = 48 * 1024 * 1024


def _mm(a, b):
    return jnp.dot(a.astype(BF16), b.astype(BF16), preferred_element_type=F32)


def _mm_nt(a, b):
    return lax.dot_general(a.astype(BF16), b.astype(BF16), (((1,), (1,)), ((), ())),
                           preferred_element_type=F32)


def _split3(x):
    hi = x.astype(BF16)
    r = x - hi.astype(F32)
    mid = r.astype(BF16)
    lo = (r - mid.astype(F32)).astype(BF16)
    return hi, mid, lo


def _softplus(x):
    return jnp.maximum(x, 0.0) + jnp.log1p(jnp.exp(-jnp.abs(x)))


def _silu(x):
    return x * jax.nn.sigmoid(x)


def _norm_proj_kernel(x_ref, g_ref, w_ref, o_ref, xn_ref):
    @pl.when(pl.program_id(1) == 0)
    def _():
        x = x_ref[...]
        ms = jnp.mean(x * x, axis=-1, keepdims=True)
        xn_ref[...] = (x * lax.rsqrt(ms + EPS) * g_ref[...]).astype(BF16)

    o_ref[...] = jnp.dot(xn_ref[...], w_ref[...], preferred_element_type=F32).astype(o_ref.dtype)


def _weight_spec(w, block, index_map, layer):
    if layer is None:
        return pl.BlockSpec(block, index_map)
    return pl.BlockSpec((None,) + block, lambda *ids: (layer,) + tuple(index_map(*ids)))


def norm_proj(x, g, w, out_dtype, tn, tm=PROJ_TM, layer=None):
    t, d = x.shape
    n = w.shape[-1]
    return pl.pallas_call(
        _norm_proj_kernel,
        grid=(t // tm, n // tn),
        in_specs=[pl.BlockSpec((tm, d), lambda i, j: (i, 0)),
                  pl.BlockSpec((1, d), lambda i, j: (0, 0)),
                  _weight_spec(w, (d, tn), lambda i, j: (0, j), layer)],
        out_specs=pl.BlockSpec((tm, tn), lambda i, j: (i, j)),
        out_shape=jax.ShapeDtypeStruct((t, n), out_dtype),
        scratch_shapes=[pltpu.VMEM((tm, d), BF16)],
        compiler_params=pltpu.CompilerParams(dimension_semantics=("arbitrary", "arbitrary"),
                                             vmem_limit_bytes=VMEM_LIMIT),
        name="norm_proj",
    )(x, g.reshape(1, d), w)


def _out_proj_kernel(a_ref, w_ref, r_ref, o_ref):
    o_ref[...] = r_ref[...] + jnp.dot(a_ref[...], w_ref[...], preferred_element_type=F32)


def _out_proj_norm_kernel(a_ref, w_ref, r_ref, g_ref, o_ref):
    y = r_ref[...] + jnp.dot(a_ref[...], w_ref[...], preferred_element_type=F32)
    ms = jnp.mean(y * y, axis=-1, keepdims=True)
    o_ref[...] = y * lax.rsqrt(ms + EPS) * g_ref[...]


def out_proj(a, w, resid, final_g=None, layer=None):
    t, k = a.shape
    n = w.shape[-1]
    in_specs = [pl.BlockSpec((PROJ_TM, k), lambda i: (i, 0)),
                _weight_spec(w, (k, n), lambda i: (0, 0), layer),
                pl.BlockSpec((PROJ_TM, n), lambda i: (i, 0))]
    args = [a, w, resid]
    body = _out_proj_kernel
    if final_g is not None:
        in_specs.append(pl.BlockSpec((1, n), lambda i: (0, 0)))
        args.append(final_g.reshape(1, n))
        body = _out_proj_norm_kernel
    return pl.pallas_call(
        body,
        grid=(t // PROJ_TM,),
        in_specs=in_specs,
        out_specs=pl.BlockSpec((PROJ_TM, n), lambda i: (i, 0)),
        out_shape=jax.ShapeDtypeStruct((t, n), F32),
        compiler_params=pltpu.CompilerParams(dimension_semantics=("arbitrary",),
                                             vmem_limit_bytes=VMEM_LIMIT),
        name="out_proj",
    )(*args)


def _gdn_in_proj_kernel(x_ref, g_ref, w_ref, cw_ref, wab_ref, alog_ref, dtb_ref,
                        o_ref, bg_ref, gct_ref, xn_ref, pre_ref, hist_ref, *, tiles_per_seq):
    i = pl.program_id(0)
    j = pl.program_id(1)
    tm, tn = o_ref.shape
    rs = GDN_PROJ_SUB
    nsub = tm // rs
    dk = GDN_DK
    dot = functools.partial(jnp.dot, preferred_element_type=F32)

    @pl.when(j == 0)
    def _():
        x = x_ref[...]
        ms = jnp.mean(x * x, axis=-1, keepdims=True)
        xn = x * lax.rsqrt(ms + EPS) * g_ref[...]
        xh = xn.astype(BF16)
        xn_ref[...] = xh
        xl = (xn - xh.astype(F32)).astype(BF16)
        w = wab_ref[...]
        wh = w.astype(BF16)
        wl = (w - wh.astype(F32)).astype(BF16)
        ab = dot(xh, wh) + (dot(xl, wh) + dot(xh, wl))
        g = -jnp.exp(alog_ref[...]) * _softplus(ab + dtb_ref[...])
        beta = jax.nn.sigmoid(ab)
        tb = GDN_BLOCK
        row = lax.broadcasted_iota(jnp.int32, (tb, tb), 0)
        col = lax.broadcasted_iota(jnp.int32, (tb, tb), 1)
        tri = jnp.where((row >= col) & ((row ^ col) < GDN_CHUNK), 1.0, 0.0).astype(BF16)
        lane = lax.broadcasted_iota(jnp.int32, (tb, LANES), 1)
        for blk in range(tm // tb):
            rows = slice(blk * tb, (blk + 1) * tb)
            hi, mid, lo = _split3(g[rows])
            gc = dot(tri, hi) + (dot(tri, mid) + dot(tri, lo))
            bg_ref[rows, :] = jnp.where(lane < GDN_HEADS, gc, beta[rows])
            gct_ref[blk] = gc.T[0:GDN_HEADS, :]

    def project(r):
        return dot(xn_ref[r * rs:(r + 1) * rs, :], w_ref[...])

    @pl.when(j == 2)
    def _():
        for r in range(nsub):
            o_ref[r * rs:(r + 1) * rs, :] = _silu(project(r))

    def conv_branch(col_tile, normalize):
        def stage(r, pre):
            if r == 0:
                first = (i % tiles_per_seq) == 0
                pre_ref[0, 0:CONV_TAIL, :] = jnp.where(first, 0.0, hist_ref[col_tile])
            else:
                pre_ref[r, 0:CONV_TAIL, :] = pre_ref[r - 1, rs:rs + CONV_TAIL, :]
            pre_ref[r, CONV_TAIL:, :] = pre

        def finish(r):
            xe = pre_ref[r]
            acc = xe[CONV_TAIL:, :] * cw_ref[CONV_K - 1:CONV_K, :]
            for tap in reversed(range(CONV_K - 1)):
                shifted = pltpu.roll(xe, CONV_K - 1 - tap, axis=0)
                acc = acc + shifted[CONV_TAIL:, :] * cw_ref[tap:tap + 1, :]
            y = _silu(acc)
            if normalize:
                groups = []
                for c in range(tn // dk):
                    yc = y[:, c * dk:(c + 1) * dk]
                    inv = lax.rsqrt(jnp.sum(yc * yc, axis=-1, keepdims=True) + EPS)
                    if c < GDN_HEADS:
                        inv = inv * (dk ** -0.5)
                    groups.append(yc * inv)
                y = jnp.concatenate(groups, axis=1)
            o_ref[r * rs:(r + 1) * rs, :] = y

        stage(0, project(0))
        for r in range(1, nsub):
            pre = project(r)
            finish(r - 1)
            stage(r, pre)
        finish(nsub - 1)
        hist_ref[col_tile] = pre_ref[nsub - 1, rs:rs + CONV_TAIL, :]

    @pl.when(j == 0)
    def _():
        conv_branch(0, True)

    @pl.when(j == 1)
    def _():
        conv_branch(1, False)


def gdn_in_proj(x, g, w, conv_w, layer, w_ab, a_log, dt_bias, seq):
    t, d = x.shape
    tm, tn = GDN_PROJ_TM, GDN_PROJ_TN
    n = 3 * tn
    assert n == GDN_MAIN_W and conv_w.shape[-1] == 2 * tn and seq % tm == 0 and tm % GDN_BLOCK == 0
    rs = GDN_PROJ_SUB
    nblk = tm // GDN_BLOCK
    lane_row = lambda v: jnp.pad(v.astype(F32), (0, LANES - v.shape[0])).reshape(1, LANES)
    return pl.pallas_call(
        functools.partial(_gdn_in_proj_kernel, tiles_per_seq=seq // tm),
        grid=(t // tm, n // tn),
        in_specs=[pl.BlockSpec((tm, d), lambda i, j: (i, 0)),
                  pl.BlockSpec((1, d), lambda i, j: (0, 0)),
                  _weight_spec(w, (d, tn), lambda i, j: (0, j), layer),
                  _weight_spec(conv_w, (CONV_K, tn), lambda i, j: (0, jnp.minimum(j, 1)), layer),
                  pl.BlockSpec((d, LANES), lambda i, j: (0, 0)),
                  pl.BlockSpec((1, LANES), lambda i, j: (0, 0)),
                  pl.BlockSpec((1, LANES), lambda i, j: (0, 0))],
        out_specs=[pl.BlockSpec((tm, tn), lambda i, j: (i, j)),
                   pl.BlockSpec((tm, LANES), lambda i, j: (i, 0)),
                   pl.BlockSpec((nblk, GDN_HEADS, GDN_BLOCK), lambda i, j: (i, 0, 0))],
        out_shape=[jax.ShapeDtypeStruct((t, n), F32),
                   jax.ShapeDtypeStruct((t, LANES), F32),
                   jax.ShapeDtypeStruct((t // GDN_BLOCK, GDN_HEADS, GDN_BLOCK), F32)],
        scratch_shapes=[pltpu.VMEM((tm, d), BF16),
                        pltpu.VMEM((tm // rs, rs + CONV_TAIL, tn), F32),
                        pltpu.VMEM((2, CONV_TAIL, tn), F32)],
        compiler_params=pltpu.CompilerParams(dimension_semantics=("arbitrary", "arbitrary"),
                                             vmem_limit_bytes=VMEM_LIMIT),
        name="gdn_in_proj",
    )(x, g.reshape(1, d), w, conv_w, w_ab, lane_row(a_log), lane_row(dt_bias))


def _unit_lower_inverse_minus_eye(mats, r):
    n = mats[0].shape[0]
    d = [jnp.where(r < INV_BASE, a, 0.0) for a in mats]
    d2 = [_mm(x, x) for x in d]
    d3 = [_mm(x, y) for x, y in zip(d, d2)]
    d4 = [_mm(y, y) for y in d2]
    e = [y - x - z for x, y, z in zip(d, d2, d3)]
    e = [x + y + _mm(x, y) for x, y in zip(e, d4)]
    b = INV_BASE
    while b < n:
        m = [jnp.where((r >= b) & (r < 2 * b), a, 0.0) for a in mats]
        x = [mi + _mm(ei, mi) for ei, mi in zip(e, m)]
        e = [ei - xi - _mm(xi, ei) for ei, xi in zip(e, x)]
        b *= 2
    return e


def _gdn_core_kernel(q_ref, k_ref, v_ref, gate_ref, bg_ref, gct_ref, og_ref, wo_ref, r_ref, x_ref, s_ref):
    hp = GDN_HEADS
    tb, c = GDN_BLOCK, GDN_CHUNK
    dk, dv = GDN_DK, GDN_DV

    @pl.when(pl.program_id(1) == 0)
    def _():
        s_ref[...] = jnp.zeros_like(s_ref)

    bg = bg_ref[...]
    lane = lax.broadcasted_iota(jnp.int32, bg.shape, 1)
    row = lax.broadcasted_iota(jnp.int32, (c, c), 0)
    col = lax.broadcasted_iota(jnp.int32, (c, c), 1)
    incl = row >= col
    strict = row > col
    r = row ^ col

    units = [(hh, ci) for ci in range(tb // c) for hh in range(hp)]
    q, k, v, gc_col, beta, gc_row = [], [], [], [], [], []
    for hh in range(hp):
        q.append(q_ref[:, hh * dk:(hh + 1) * dk])
        k.append(k_ref[:, hh * dk:(hh + 1) * dk])
        v.append(v_ref[:, hh * dv:(hh + 1) * dv])
        gc_col.append(jnp.sum(jnp.where(lane == hh, bg, 0.0), axis=1, keepdims=True))
        beta.append(jnp.sum(jnp.where(lane == hh + GDN_HEADS, bg, 0.0), axis=1, keepdims=True))
        gc_row.append(gct_ref[hh:hh + 1, :])
    pick = lambda arrs: [arrs[hh][ci * c:(ci + 1) * c] for hh, ci in units]
    q, k, v, gc_col, beta = pick(q), pick(k), pick(v), pick(gc_col), pick(beta)
    gc_row = [gc_row[hh][:, ci * c:(ci + 1) * c] for hh, ci in units]
    gc_last = [g[:, c - 1:c] for g in gc_row]

    k_t = [x.T for x in k]
    decay = [jnp.where(incl, jnp.exp(jnp.where(incl, gcc - gcr, 0.0)), 0.0)
             for gcc, gcr in zip(gc_col, gc_row)]
    kb = [ki * bi for ki, bi in zip(k, beta)]
    a = [jnp.where(strict, _mm(kbi, kti) * di, 0.0) for kbi, kti, di in zip(kb, k_t, decay)]
    qk = [_mm(qi, kti) * di for qi, kti, di in zip(q, k_t, decay)]
    egc = [jnp.exp(g) for g in gc_col]
    rhs = [jnp.concatenate([kbi * ei, vi * bi], axis=1) for kbi, ei, vi, bi in zip(kb, egc, v, beta)]
    e = _unit_lower_inverse_minus_eye(a, r)
    sol = [x + _mm(ei, x) for ei, x in zip(e, rhs)]
    q_dec = [qi * ei for qi, ei in zip(q, egc)]
    k_dec_t = [kti * jnp.exp(gl - gcr) for kti, gl, gcr in zip(k_t, gc_last, gc_row)]
    both = [_mm(jnp.concatenate([kdt, qki], axis=0), x) for kdt, qki, x in zip(k_dec_t, qk, sol)]
    s_coef = [jnp.concatenate([x[0:dk, 0:dk], qd - x[dk:, 0:dk]], axis=0) for x, qd in zip(both, q_dec)]
    s_add = [x[0:dk, dk:] for x in both]
    o_add = [x[dk:, dk:] for x in both]

    state = [s_ref[hh] for hh in range(hp)]
    outs = [[] for _ in range(hp)]
    for u, (hh, ci) in enumerate(units):
        ps = _mm(s_coef[u], state[hh])
        outs[hh].append(o_add[u] + ps[dk:])
        state[hh] = state[hh] * jnp.exp(gc_last[u]) + (s_add[u] - ps[0:dk])
    gated = []
    for hh in range(hp):
        s_ref[hh] = state[hh]
        o = jnp.concatenate(outs[hh], axis=0)
        on = o * lax.rsqrt(jnp.mean(o * o, axis=-1, keepdims=True) + EPS) * og_ref[...]
        gated.append((on * gate_ref[:, hh * dv:(hh + 1) * dv]).astype(BF16))
    x_ref[...] = r_ref[...] + jnp.dot(jnp.concatenate(gated, axis=1), wo_ref[...],
                                      preferred_element_type=F32)


def gdn_core(proj, bg, gct, out_g, w_out, layer, resid, batch):
    t, d = resid.shape
    tb = GDN_BLOCK
    nt = t // batch // tb
    dk, dv, nh = GDN_DK, GDN_DV, GDN_HEADS
    wk, wv = nh * dk, nh * dv
    rowblk = lambda b, i: b * nt + i
    return pl.pallas_call(
        _gdn_core_kernel,
        grid=(batch, nt),
        in_specs=[pl.BlockSpec((tb, wk), lambda b, i: (rowblk(b, i), 0)),
                  pl.BlockSpec((tb, wk), lambda b, i: (rowblk(b, i), 1)),
                  pl.BlockSpec((tb, wv), lambda b, i: (rowblk(b, i), 1)),
                  pl.BlockSpec((tb, wv), lambda b, i: (rowblk(b, i), 2)),
                  pl.BlockSpec((tb, LANES), lambda b, i: (rowblk(b, i), 0)),
                  pl.BlockSpec((None, nh, tb), lambda b, i: (rowblk(b, i), 0, 0)),
                  pl.BlockSpec((1, dv), lambda b, i: (0, 0)),
                  _weight_spec(w_out, (wv, d), lambda b, i: (0, 0), layer),
                  pl.BlockSpec((tb, d), lambda b, i: (rowblk(b, i), 0))],
        out_specs=pl.BlockSpec((tb, d), lambda b, i: (rowblk(b, i), 0)),
        out_shape=jax.ShapeDtypeStruct((t, d), F32),
        scratch_shapes=[pltpu.VMEM((nh, dk, dv), F32)],
        compiler_params=pltpu.CompilerParams(
            dimension_semantics=("arbitrary", "arbitrary"),
            vmem_limit_bytes=VMEM_LIMIT),
        name="gdn_core",
    )(proj, proj, proj, proj, bg, gct, out_g.reshape(1, dv), w_out, resid)


def gdn_layer(x, batch, layer, norm_g, w_in, w_in_bf16, conv_w, a_log, dt_bias, out_g, w_out_bf16):
    w_ab = jnp.pad(w_in[layer][:, GDN_MAIN_W:], ((0, 0), (0, LANES - 2 * GDN_HEADS)))
    proj, bg, gct = gdn_in_proj(x, norm_g, w_in_bf16, conv_w, layer, w_ab, a_log, dt_bias, x.shape[0] // batch)
    return gdn_core(proj, bg, gct, out_g, w_out_bf16, layer, x, batch)


def _sb_core_kernel(q_ref, k_ref, v_ref, z_ref, o_ref, acc_ref, later_ref):
    blk, nqb = SB_BLOCK, SB_QBLOCKS
    i0 = pl.program_id(2) * nqb
    row = lax.broadcasted_iota(jnp.int32, (blk, blk), 0)
    col = lax.broadcasted_iota(jnp.int32, (blk, blk), 1)
    suffix = jnp.where(row >= col, 1.0, 0.0).astype(BF16)
    dot = functools.partial(jnp.dot, preferred_element_type=F32)
    rows = [slice(c * blk, (c + 1) * blk) for c in range(nqb)]
    q = [(q_ref[r, :] * (SB_DH ** -0.5)).astype(BF16) for r in rows]

    def scores(qc, j):
        start = pl.multiple_of(j * blk, blk)
        z = _mm_nt(qc, k_ref[pl.ds(start, blk), :])
        return z, -(jnp.maximum(z, 0.0) + jnp.log(1.0 + jnp.exp(-jnp.abs(z))))

    def suffix_sums(log1m):
        hi, mid, lo = _split3(log1m)
        return dot(hi, suffix) + (dot(mid, suffix) + dot(lo, suffix))

    def weighted_values(j, attn):
        start = pl.multiple_of(j * blk, blk)
        return dot(attn.astype(BF16), v_ref[pl.ds(start, blk), :])

    idx = [i0 + c for c in range(nqb)]
    has_prev = [ix > 0 for ix in idx]
    jp = [jnp.maximum(ix - 1, 0) for ix in idx]
    valid = col < row
    zl_d = [scores(qc, ix) for qc, ix in zip(q, idx)]
    zl_p = [scores(qc, j) for qc, j in zip(q, jp)]
    rc_d = [suffix_sums(jnp.where(valid, l, 0.0)) for _, l in zl_d]
    rc_p = [suffix_sums(jnp.where(hp, l, 0.0)) for (_, l), hp in zip(zl_p, has_prev)]
    later_d = [x[:, 0:1] for x in rc_d]
    attn_d = [jnp.where(valid, jnp.exp(z + rc), 0.0) for (z, _), rc in zip(zl_d, rc_d)]
    attn_p = [jnp.where(hp, jnp.exp(z + (rc + ld)), 0.0)
              for (z, _), rc, ld, hp in zip(zl_p, rc_p, later_d, has_prev)]
    for c, r in enumerate(rows):
        acc_ref[r, :] = weighted_values(idx[c], attn_d[c]) + weighted_values(jp[c], attn_p[c])
        later_ref[r, :] = later_d[c] + rc_p[c][:, 0:1]

    for c, r in enumerate(rows):
        def cond(carry):
            j, worst = carry
            return jnp.logical_and(j >= 0, worst > -SB_SKIP)

        def body(carry, c=c, r=r):
            j, _ = carry
            z, log1m = scores(q[c], j)
            rc_in = suffix_sums(log1m)
            later = later_ref[r, :]
            acc_ref[r, :] += weighted_values(j, jnp.exp(z + (rc_in + later)))
            later_ref[r, :] = later + rc_in[:, 0:1]
            return j - 1, jnp.max(later_ref[r, :])

        lax.while_loop(cond, body, (idx[c] - 2, jnp.max(later_ref[r, :])))
    o_ref[...] = (acc_ref[...] * _silu(z_ref[...])).astype(o_ref.dtype)


def sb_core(proj, kv, batch):
    t = proj.shape[0]
    s = t // batch
    qrows = SB_BLOCK * SB_QBLOCKS
    nq = s // qrows
    nh, dh = SB_HEADS, SB_DH
    return pl.pallas_call(
        _sb_core_kernel,
        grid=(batch, nh, nq),
        in_specs=[pl.BlockSpec((qrows, dh), lambda b, h, i: (b * nq + i, h)),
                  pl.BlockSpec((s, dh), lambda b, h, i: (b, h)),
                  pl.BlockSpec((s, dh), lambda b, h, i: (b, nh + h)),
                  pl.BlockSpec((qrows, dh), lambda b, h, i: (b * nq + i, nh + h))],
        out_specs=pl.BlockSpec((qrows, dh), lambda b, h, i: (b * nq + i, h)),
        out_shape=jax.ShapeDtypeStruct((t, nh * dh), BF16),
        scratch_shapes=[pltpu.VMEM((qrows, dh), F32),
                        pltpu.VMEM((qrows, 1), F32)],
        compiler_params=pltpu.CompilerParams(
            dimension_semantics=("arbitrary", "arbitrary", "arbitrary"),
            vmem_limit_bytes=VMEM_LIMIT),
        name="sb_core",
    )(proj, kv, kv, proj)


def sb_layer(x, batch, layer, norm_g, w_in_bf16, w_out_bf16, kv, final_g):
    proj = norm_proj(x, norm_g, w_in_bf16, F32, tn=2 * SB_W, layer=layer)
    o = sb_core(proj, kv, batch)
    return out_proj(o, w_out_bf16, x, final_g, layer=layer)


def kernel(x, a_norm, a_w_in, a_conv, a_A_log, a_dt_bias, a_out_norm, a_w_out, kv_norm, w_kv, b_norm,
           b_w_in, b_w_out, final_norm):
    batch, seq, d = x.shape
    h = x.reshape(batch * seq, d)
    a_w_in_bf16, a_w_out_bf16 = a_w_in.astype(BF16), a_w_out.astype(BF16)
    b_w_in_bf16, b_w_out_bf16 = b_w_in.astype(BF16), b_w_out.astype(BF16)
    for l in range(a_w_in.shape[0]):
        h = gdn_layer(h, batch, l, a_norm[l], a_w_in, a_w_in_bf16, a_conv, a_A_log[l], a_dt_bias[l],
                      a_out_norm[l], a_w_out_bf16)
    kv = norm_proj(h, kv_norm, w_kv.astype(BF16), BF16, tn=2 * SB_W)
    n_b = b_w_in.shape[0]
    for j in range(n_b):
        h = sb_layer(h, batch, j, b_norm[j], b_w_in_bf16, b_w_out_bf16, kv,
                     final_norm if j == n_b - 1 else None)
    return h.reshape(batch, seq, d)
```

```python
import functools

import jax
import jax.numpy as jnp
from jax import lax
from jax.experimental import pallas as pl
from jax.experimental.pallas import tpu as pltpu

F32 = jnp.float32
BF16 = jnp.bfloat16

EPS = 1e-6
LANES = 128

GDN_HEADS = 8
GDN_DK = 128
GDN_DV = 256
GDN_KEY_W = GDN_HEADS * GDN_DK
GDN_VAL_W = GDN_HEADS * GDN_DV
GDN_MAIN_W = 2 * GDN_KEY_W + 2 * GDN_VAL_W
CONV_K = 4
GDN_BLOCK = 256
GDN_CHUNK = 128
CONV_TAIL = 8
GDN_PROJ_TN = 2048
INV_BASE = 8

SB_HEADS = 4
SB_DH = 256
SB_W = SB_HEADS * SB_DH
SB_BLOCK = 256
SB_QBLOCKS = 4
SB_SKIP = 110.0

PROJ_TM = 512
VMEM_LIMIT = 56 * 1024 * 1024


def _mm(a, b):
    return jnp.dot(a.astype(BF16), b.astype(BF16), preferred_element_type=F32)


def _mm_nt(a, b):
    return lax.dot_general(a.astype(BF16), b.astype(BF16), (((1,), (1,)), ((), ())),
                           preferred_element_type=F32)


def _split3(x):
    hi = x.astype(BF16)
    r = x - hi.astype(F32)
    mid = r.astype(BF16)
    lo = (r - mid.astype(F32)).astype(BF16)
    return hi, mid, lo


def _softplus(x):
    return jnp.maximum(x, 0.0) + jnp.log1p(jnp.exp(-jnp.abs(x)))


def _silu(x):
    return x * jax.nn.sigmoid(x)


def _norm_proj_kernel(x_ref, g_ref, w_ref, o_ref, xn_ref):
    @pl.when(pl.program_id(1) == 0)
    def _():
        x = x_ref[...]
        ms = jnp.mean(x * x, axis=-1, keepdims=True)
        xn_ref[...] = (x * lax.rsqrt(ms + EPS) * g_ref[...]).astype(BF16)

    o_ref[...] = jnp.dot(xn_ref[...], w_ref[...], preferred_element_type=F32).astype(o_ref.dtype)


def _weight_spec(w, block, index_map, layer):
    if layer is None:
        return pl.BlockSpec(block, index_map)
    return pl.BlockSpec((None,) + block, lambda *ids: (layer,) + tuple(index_map(*ids)))


def norm_proj(x, g, w, out_dtype, tn, tm=PROJ_TM, layer=None):
    t, d = x.shape
    n = w.shape[-1]
    return pl.pallas_call(
        _norm_proj_kernel,
        grid=(t // tm, n // tn),
        in_specs=[pl.BlockSpec((tm, d), lambda i, j: (i, 0)),
                  pl.BlockSpec((1, d), lambda i, j: (0, 0)),
                  _weight_spec(w, (d, tn), lambda i, j: (0, j), layer)],
        out_specs=pl.BlockSpec((tm, tn), lambda i, j: (i, j)),
        out_shape=jax.ShapeDtypeStruct((t, n), out_dtype),
        scratch_shapes=[pltpu.VMEM((tm, d), BF16)],
        compiler_params=pltpu.CompilerParams(dimension_semantics=("arbitrary", "arbitrary"),
                                             vmem_limit_bytes=VMEM_LIMIT),
        name="norm_proj",
    )(x, g.reshape(1, d), w)


def _out_proj_kernel(a_ref, w_ref, r_ref, o_ref):
    o_ref[...] = r_ref[...] + jnp.dot(a_ref[...], w_ref[...], preferred_element_type=F32)


def _out_proj_norm_kernel(a_ref, w_ref, r_ref, g_ref, o_ref):
    y = r_ref[...] + jnp.dot(a_ref[...], w_ref[...], preferred_element_type=F32)
    ms = jnp.mean(y * y, axis=-1, keepdims=True)
    o_ref[...] = y * lax.rsqrt(ms + EPS) * g_ref[...]


def out_proj(a, w, resid, final_g=None, layer=None):
    t, k = a.shape
    n = w.shape[-1]
    in_specs = [pl.BlockSpec((PROJ_TM, k), lambda i: (i, 0)),
                _weight_spec(w, (k, n), lambda i: (0, 0), layer),
                pl.BlockSpec((PROJ_TM, n), lambda i: (i, 0))]
    args = [a, w, resid]
    body = _out_proj_kernel
    if final_g is not None:
        in_specs.append(pl.BlockSpec((1, n), lambda i: (0, 0)))
        args.append(final_g.reshape(1, n))
        body = _out_proj_norm_kernel
    return pl.pallas_call(
        body,
        grid=(t // PROJ_TM,),
        in_specs=in_specs,
        out_specs=pl.BlockSpec((PROJ_TM, n), lambda i: (i, 0)),
        out_shape=jax.ShapeDtypeStruct((t, n), F32),
        compiler_params=pltpu.CompilerParams(dimension_semantics=("arbitrary",),
                                             vmem_limit_bytes=VMEM_LIMIT),
        name="out_proj",
    )(*args)


def _gdn_in_proj_kernel(x_ref, g_ref, w_ref, cw_ref, wab_ref, alog_ref, dtb_ref,
                        o_ref, bg_ref, gct_ref, pre_ref, hist_ref, *, tiles_per_seq):
    i = pl.program_id(0)
    tm = x_ref.shape[0]
    tn = GDN_PROJ_TN
    dk = GDN_DK
    dot = functools.partial(jnp.dot, preferred_element_type=F32)

    x = x_ref[...]
    ms = jnp.mean(x * x, axis=-1, keepdims=True)
    xn = x * lax.rsqrt(ms + EPS) * g_ref[...]
    xh = xn.astype(BF16)

    def project(c):
        return dot(xh, w_ref[:, c * tn:(c + 1) * tn])

    first = (i % tiles_per_seq) == 0

    def stage(c, pre):
        pre_ref[c, 0:CONV_TAIL, :] = jnp.where(first, 0.0, hist_ref[c])
        pre_ref[c, CONV_TAIL:, :] = pre
        hist_ref[c] = pre[tm - CONV_TAIL:, :]

    def finish(c, normalize):
        xe = pre_ref[c]
        cw = cw_ref[:, c * tn:(c + 1) * tn]
        acc = xe[CONV_TAIL:, :] * cw[CONV_K - 1:CONV_K, :]
        for tap in reversed(range(CONV_K - 1)):
            shifted = pltpu.roll(xe, CONV_K - 1 - tap, axis=0)
            acc = acc + shifted[CONV_TAIL:, :] * cw[tap:tap + 1, :]
        y = _silu(acc)
        if normalize:
            groups = []
            for h in range(tn // dk):
                yh = y[:, h * dk:(h + 1) * dk]
                inv = lax.rsqrt(jnp.sum(yh * yh, axis=-1, keepdims=True) + EPS)
                if h < GDN_HEADS:
                    inv = inv * (dk ** -0.5)
                groups.append(yh * inv)
            y = jnp.concatenate(groups, axis=1)
        o_ref[:, c * tn:(c + 1) * tn] = y

    stage(0, project(0))

    xl = (xn - xh.astype(F32)).astype(BF16)
    w = wab_ref[...]
    wh = w.astype(BF16)
    wl = (w - wh.astype(F32)).astype(BF16)
    ab = dot(xh, wh) + (dot(xl, wh) + dot(xh, wl))
    g = -jnp.exp(alog_ref[...]) * _softplus(ab + dtb_ref[...])
    beta = jax.nn.sigmoid(ab)
    row = lax.broadcasted_iota(jnp.int32, (tm, tm), 0)
    col = lax.broadcasted_iota(jnp.int32, (tm, tm), 1)
    tri = jnp.where((row >= col) & ((row ^ col) < GDN_CHUNK), 1.0, 0.0).astype(BF16)
    hi, mid, lo = _split3(g)
    gc = dot(tri, hi) + (dot(tri, mid) + dot(tri, lo))
    lane = lax.broadcasted_iota(jnp.int32, (tm, LANES), 1)
    bg_ref[...] = jnp.where(lane < GDN_HEADS, gc, beta)
    gct_ref[...] = gc.T[0:GDN_HEADS, :]

    pre = project(1)
    finish(0, True)
    stage(1, pre)
    pre = project(2)
    finish(1, False)
    o_ref[:, 2 * tn:] = _silu(pre)


def gdn_in_proj(x, g, w, conv_w, layer, w_ab, a_log, dt_bias, seq):
    t, d = x.shape
    tm, tn = GDN_BLOCK, GDN_PROJ_TN
    n = 3 * tn
    assert n == GDN_MAIN_W and conv_w.shape[-1] == 2 * tn and seq % tm == 0
    lane_row = lambda v: jnp.pad(v.astype(F32), (0, LANES - v.shape[0])).reshape(1, LANES)
    return pl.pallas_call(
        functools.partial(_gdn_in_proj_kernel, tiles_per_seq=seq // tm),
        grid=(t // tm,),
        in_specs=[pl.BlockSpec((tm, d), lambda i: (i, 0)),
                  pl.BlockSpec((1, d), lambda i: (0, 0)),
                  _weight_spec(w, (d, n), lambda i: (0, 0), layer),
                  _weight_spec(conv_w, (CONV_K, 2 * tn), lambda i: (0, 0), layer),
                  pl.BlockSpec((d, LANES), lambda i: (0, 0)),
                  pl.BlockSpec((1, LANES), lambda i: (0, 0)),
                  pl.BlockSpec((1, LANES), lambda i: (0, 0))],
        out_specs=[pl.BlockSpec((tm, n), lambda i: (i, 0)),
                   pl.BlockSpec((tm, LANES), lambda i: (i, 0)),
                   pl.BlockSpec((None, GDN_HEADS, GDN_BLOCK), lambda i: (i, 0, 0))],
        out_shape=[jax.ShapeDtypeStruct((t, n), F32),
                   jax.ShapeDtypeStruct((t, LANES), F32),
                   jax.ShapeDtypeStruct((t // GDN_BLOCK, GDN_HEADS, GDN_BLOCK), F32)],
        scratch_shapes=[pltpu.VMEM((2, tm + CONV_TAIL, tn), F32),
                        pltpu.VMEM((2, CONV_TAIL, tn), F32)],
        compiler_params=pltpu.CompilerParams(dimension_semantics=("arbitrary",),
                                             vmem_limit_bytes=VMEM_LIMIT),
        name="gdn_in_proj",
    )(x, g.reshape(1, d), w, conv_w, w_ab, lane_row(a_log), lane_row(dt_bias))


def _unit_lower_inverse_minus_eye(mats, r):
    n = mats[0].shape[0]
    d = [jnp.where(r < INV_BASE, a, 0.0) for a in mats]
    d2 = [_mm(x, x) for x in d]
    d3 = [_mm(x, y) for x, y in zip(d, d2)]
    d4 = [_mm(y, y) for y in d2]
    e = [y - x - z for x, y, z in zip(d, d2, d3)]
    e = [x + y + _mm(x, y) for x, y in zip(e, d4)]
    b = INV_BASE
    while b < n:
        m = [jnp.where((r >= b) & (r < 2 * b), a, 0.0) for a in mats]
        x = [mi + _mm(ei, mi) for ei, mi in zip(e, m)]
        e = [ei - xi - _mm(xi, ei) for ei, xi in zip(e, x)]
        b *= 2
    return e


def _gdn_core_kernel(q_ref, k_ref, v_ref, gate_ref, bg_ref, gct_ref, og_ref, wo_ref, r_ref, x_ref, s_ref):
    hp = GDN_HEADS
    tb, c = GDN_BLOCK, GDN_CHUNK
    dk, dv = GDN_DK, GDN_DV

    @pl.when(pl.program_id(1) == 0)
    def _():
        s_ref[...] = jnp.zeros_like(s_ref)

    bg = bg_ref[...]
    lane = lax.broadcasted_iota(jnp.int32, bg.shape, 1)
    row = lax.broadcasted_iota(jnp.int32, (c, c), 0)
    col = lax.broadcasted_iota(jnp.int32, (c, c), 1)
    incl = row >= col
    strict = row > col
    r = row ^ col

    units = [(hh, ci) for ci in range(tb // c) for hh in range(hp)]
    q, k, v, gc_col, beta, gc_row = [], [], [], [], [], []
    for hh in range(hp):
        q.append(q_ref[:, hh * dk:(hh + 1) * dk])
        k.append(k_ref[:, hh * dk:(hh + 1) * dk])
        v.append(v_ref[:, hh * dv:(hh + 1) * dv])
        gc_col.append(jnp.sum(jnp.where(lane == hh, bg, 0.0), axis=1, keepdims=True))
        beta.append(jnp.sum(jnp.where(lane == hh + GDN_HEADS, bg, 0.0), axis=1, keepdims=True))
        gc_row.append(gct_ref[hh:hh + 1, :])
    pick = lambda arrs: [arrs[hh][ci * c:(ci + 1) * c] for hh, ci in units]
    q, k, v, gc_col, beta = pick(q), pick(k), pick(v), pick(gc_col), pick(beta)
    gc_row = [gc_row[hh][:, ci * c:(ci + 1) * c] for hh, ci in units]
    gc_last = [g[:, c - 1:c] for g in gc_row]

    k_t = [x.T for x in k]
    decay = [jnp.where(incl, jnp.exp(jnp.where(incl, gcc - gcr, 0.0)), 0.0)
             for gcc, gcr in zip(gc_col, gc_row)]
    kb = [ki * bi for ki, bi in zip(k, beta)]
    a = [jnp.where(strict, _mm(kbi, kti) * di, 0.0) for kbi, kti, di in zip(kb, k_t, decay)]
    qk = [_mm(qi, kti) * di for qi, kti, di in zip(q, k_t, decay)]
    egc = [jnp.exp(g) for g in gc_col]
    rhs = [jnp.concatenate([kbi * ei, vi * bi], axis=1) for kbi, ei, vi, bi in zip(kb, egc, v, beta)]
    e = _unit_lower_inverse_minus_eye(a, r)
    sol = [x + _mm(ei, x) for ei, x in zip(e, rhs)]
    q_dec = [qi * ei for qi, ei in zip(q, egc)]
    k_dec_t = [kti * jnp.exp(gl - gcr) for kti, gl, gcr in zip(k_t, gc_last, gc_row)]
    both = [_mm(jnp.concatenate([kdt, qki], axis=0), x) for kdt, qki, x in zip(k_dec_t, qk, sol)]
    s_coef = [jnp.concatenate([x[0:dk, 0:dk], qd - x[dk:, 0:dk]], axis=0) for x, qd in zip(both, q_dec)]
    s_add = [x[0:dk, dk:] for x in both]
    o_add = [x[dk:, dk:] for x in both]

    state = [s_ref[hh] for hh in range(hp)]
    outs = [[] for _ in range(hp)]
    for u, (hh, ci) in enumerate(units):
        ps = _mm(s_coef[u], state[hh])
        outs[hh].append(o_add[u] + ps[dk:])
        state[hh] = state[hh] * jnp.exp(gc_last[u]) + (s_add[u] - ps[0:dk])
    gated = []
    for hh in range(hp):
        s_ref[hh] = state[hh]
        o = jnp.concatenate(outs[hh], axis=0)
        on = o * lax.rsqrt(jnp.mean(o * o, axis=-1, keepdims=True) + EPS) * og_ref[...]
        gated.append((on * gate_ref[:, hh * dv:(hh + 1) * dv]).astype(BF16))
    x_ref[...] = r_ref[...] + jnp.dot(jnp.concatenate(gated, axis=1), wo_ref[...],
                                      preferred_element_type=F32)


def gdn_core(proj, bg, gct, out_g, w_out, layer, resid, batch):
    t, d = resid.shape
    tb = GDN_BLOCK
    nt = t // batch // tb
    dk, dv, nh = GDN_DK, GDN_DV, GDN_HEADS
    wk, wv = nh * dk, nh * dv
    rowblk = lambda b, i: b * nt + i
    return pl.pallas_call(
        _gdn_core_kernel,
        grid=(batch, nt),
        in_specs=[pl.BlockSpec((tb, wk), lambda b, i: (rowblk(b, i), 0)),
                  pl.BlockSpec((tb, wk), lambda b, i: (rowblk(b, i), 1)),
                  pl.BlockSpec((tb, wv), lambda b, i: (rowblk(b, i), 1)),
                  pl.BlockSpec((tb, wv), lambda b, i: (rowblk(b, i), 2)),
                  pl.BlockSpec((tb, LANES), lambda b, i: (rowblk(b, i), 0)),
                  pl.BlockSpec((None, nh, tb), lambda b, i: (rowblk(b, i), 0, 0)),
                  pl.BlockSpec((1, dv), lambda b, i: (0, 0)),
                  _weight_spec(w_out, (wv, d), lambda b, i: (0, 0), layer),
                  pl.BlockSpec((tb, d), lambda b, i: (rowblk(b, i), 0))],
        out_specs=pl.BlockSpec((tb, d), lambda b, i: (rowblk(b, i), 0)),
        out_shape=jax.ShapeDtypeStruct((t, d), F32),
        scratch_shapes=[pltpu.VMEM((nh, dk, dv), F32)],
        compiler_params=pltpu.CompilerParams(
            dimension_semantics=("arbitrary", "arbitrary"),
            vmem_limit_bytes=VMEM_LIMIT),
        name="gdn_core",
    )(proj, proj, proj, proj, bg, gct, out_g.reshape(1, dv), w_out, resid)


def gdn_layer(x, batch, layer, norm_g, w_in, w_in_bf16, conv_w, a_log, dt_bias, out_g, w_out_bf16):
    w_ab = jnp.pad(w_in[layer][:, GDN_MAIN_W:], ((0, 0), (0, LANES - 2 * GDN_HEADS)))
    proj, bg, gct = gdn_in_proj(x, norm_g, w_in_bf16, conv_w, layer, w_ab, a_log, dt_bias, x.shape[0] // batch)
    return gdn_core(proj, bg, gct, out_g, w_out_bf16, layer, x, batch)


def _sb_core_kernel(q_ref, k_ref, v_ref, z_ref, o_ref, acc_ref, later_ref):
    blk, nqb = SB_BLOCK, SB_QBLOCKS
    i0 = pl.program_id(2) * nqb
    row = lax.broadcasted_iota(jnp.int32, (blk, blk), 0)
    col = lax.broadcasted_iota(jnp.int32, (blk, blk), 1)
    suffix = jnp.where(row >= col, 1.0, 0.0).astype(BF16)
    dot = functools.partial(jnp.dot, preferred_element_type=F32)
    rows = [slice(c * blk, (c + 1) * blk) for c in range(nqb)]
    q = [(q_ref[r, :] * (SB_DH ** -0.5)).astype(BF16) for r in rows]

    def scores(qc, j):
        start = pl.multiple_of(j * blk, blk)
        z = _mm_nt(qc, k_ref[pl.ds(start, blk), :])
        return z, -(jnp.maximum(z, 0.0) + jnp.log(1.0 + jnp.exp(-jnp.abs(z))))

    def suffix_sums(log1m):
        hi, mid, lo = _split3(log1m)
        return dot(hi, suffix) + (dot(mid, suffix) + dot(lo, suffix))

    def weighted_values(j, attn):
        start = pl.multiple_of(j * blk, blk)
        return dot(attn.astype(BF16), v_ref[pl.ds(start, blk), :])

    idx = [i0 + c for c in range(nqb)]
    has_prev = [ix > 0 for ix in idx]
    jp = [jnp.maximum(ix - 1, 0) for ix in idx]
    valid = col < row
    zl_d = [scores(qc, ix) for qc, ix in zip(q, idx)]
    zl_p = [scores(qc, j) for qc, j in zip(q, jp)]
    rc_d = [suffix_sums(jnp.where(valid, l, 0.0)) for _, l in zl_d]
    rc_p = [suffix_sums(jnp.where(hp, l, 0.0)) for (_, l), hp in zip(zl_p, has_prev)]
    later_d = [x[:, 0:1] for x in rc_d]
    attn_d = [jnp.where(valid, jnp.exp(z + rc), 0.0) for (z, _), rc in zip(zl_d, rc_d)]
    attn_p = [jnp.where(hp, jnp.exp(z + (rc + ld)), 0.0)
              for (z, _), rc, ld, hp in zip(zl_p, rc_p, later_d, has_prev)]
    for c, r in enumerate(rows):
        acc_ref[r, :] = weighted_values(idx[c], attn_d[c]) + weighted_values(jp[c], attn_p[c])
        later_ref[r, :] = later_d[c] + rc_p[c][:, 0:1]

    for c, r in enumerate(rows):
        def cond(carry):
            j, worst = carry
            return jnp.logical_and(j >= 0, worst > -SB_SKIP)

        def body(carry, c=c, r=r):
            j, _ = carry
            z, log1m = scores(q[c], j)
            rc_in = suffix_sums(log1m)
            later = later_ref[r, :]
            acc_ref[r, :] += weighted_values(j, jnp.exp(z + (rc_in + later)))
            later_ref[r, :] = later + rc_in[:, 0:1]
            return j - 1, jnp.max(later_ref[r, :])

        lax.while_loop(cond, body, (idx[c] - 2, jnp.max(later_ref[r, :])))
    o_ref[...] = (acc_ref[...] * _silu(z_ref[...])).astype(o_ref.dtype)


def sb_core(proj, kv, batch):
    t = proj.shape[0]
    s = t // batch
    qrows = SB_BLOCK * SB_QBLOCKS
    nq = s // qrows
    nh, dh = SB_HEADS, SB_DH
    return pl.pallas_call(
        _sb_core_kernel,
        grid=(batch, nh, nq),
        in_specs=[pl.BlockSpec((qrows, dh), lambda b, h, i: (b * nq + i, h)),
                  pl.BlockSpec((s, dh), lambda b, h, i: (b, h)),
                  pl.BlockSpec((s, dh), lambda b, h, i: (b, nh + h)),
                  pl.BlockSpec((qrows, dh), lambda b, h, i: (b * nq + i, nh + h))],
        out_specs=pl.BlockSpec((qrows, dh), lambda b, h, i: (b * nq + i, h)),
        out_shape=jax.ShapeDtypeStruct((t, nh * dh), BF16),
        scratch_shapes=[pltpu.VMEM((qrows, dh), F32),
                        pltpu.VMEM((qrows, 1), F32)],
        compiler_params=pltpu.CompilerParams(
            dimension_semantics=("arbitrary", "arbitrary", "arbitrary"),
            vmem_limit_bytes=VMEM_LIMIT),
        name="sb_core",
    )(proj, kv, kv, proj)


def sb_layer(x, batch, layer, norm_g, w_in_bf16, w_out_bf16, kv, final_g):
    proj = norm_proj(x, norm_g, w_in_bf16, F32, tn=2 * SB_W, layer=layer)
    o = sb_core(proj, kv, batch)
    return out_proj(o, w_out_bf16, x, final_g, layer=layer)


def kernel(x, a_norm, a_w_in, a_conv, a_A_log, a_dt_bias, a_out_norm, a_w_out, kv_norm, w_kv, b_norm,
           b_w_in, b_w_out, final_norm):
    batch, seq, d = x.shape
    h = x.reshape(batch * seq, d)
    a_w_in_bf16, a_w_out_bf16 = a_w_in.astype(BF16), a_w_out.astype(BF16)
    b_w_in_bf16, b_w_out_bf16 = b_w_in.astype(BF16), b_w_out.astype(BF16)
    for l in range(a_w_in.shape[0]):
        h = gdn_layer(h, batch, l, a_norm[l], a_w_in, a_w_in_bf16, a_conv, a_A_log[l], a_dt_bias[l],
                      a_out_norm[l], a_w_out_bf16)
    kv = norm_proj(h, kv_norm, w_kv.astype(BF16), BF16, tn=2 * SB_W)
    n_b = b_w_in.shape[0]
    for j in range(n_b):
        h = sb_layer(h, batch, j, b_norm[j], b_w_in_bf16, b_w_out_bf16, kv,
                     final_norm if j == n_b - 1 else None)
    return h.reshape(batch, seq, d)
```

```python
import functools

import jax
import jax.numpy as jnp
from jax import lax
from jax.experimental import pallas as pl
from jax.experimental.pallas import tpu as pltpu

F32 = jnp.float32
BF16 = jnp.bfloat16

EPS = 1e-6
LANES = 128

GDN_HEADS = 8
GDN_DK = 128
GDN_DV = 256
GDN_KEY_W = GDN_HEADS * GDN_DK
GDN_VAL_W = GDN_HEADS * GDN_DV
GDN_MAIN_W = 2 * GDN_KEY_W + 2 * GDN_VAL_W
CONV_K = 4
GDN_BLOCK = 256
GDN_CHUNK = 128
CONV_TAIL = 8
GDN_PROJ_TN = 2048
INV_BASE = 8

SB_HEADS = 4
SB_DH = 256
SB_W = SB_HEADS * SB_DH
SB_BLOCK = 256
SB_QBLOCKS = 4
SB_SKIP = 110.0

PROJ_TM = 1024
VMEM_LIMIT = 56 * 1024 * 1024


def _mm(a, b):
    return jnp.dot(a.astype(BF16), b.astype(BF16), preferred_element_type=F32)


def _mm_nt(a, b):
    return lax.dot_general(a.astype(BF16), b.astype(BF16), (((1,), (1,)), ((), ())),
                           preferred_element_type=F32)


def _split3(x):
    hi = x.astype(BF16)
    r = x - hi.astype(F32)
    mid = r.astype(BF16)
    lo = (r - mid.astype(F32)).astype(BF16)
    return hi, mid, lo


def _softplus(x):
    return jnp.maximum(x, 0.0) + jnp.log1p(jnp.exp(-jnp.abs(x)))


def _silu(x):
    return x * jax.nn.sigmoid(x)


def _norm_proj_kernel(x_ref, g_ref, w_ref, o_ref, xn_ref):
    @pl.when(pl.program_id(1) == 0)
    def _():
        x = x_ref[...]
        ms = jnp.mean(x * x, axis=-1, keepdims=True)
        xn_ref[...] = (x * lax.rsqrt(ms + EPS) * g_ref[...]).astype(BF16)

    o_ref[...] = jnp.dot(xn_ref[...], w_ref[...], preferred_element_type=F32).astype(o_ref.dtype)


def _weight_spec(w, block, index_map, layer):
    if layer is None:
        return pl.BlockSpec(block, index_map)
    return pl.BlockSpec((None,) + block, lambda *ids: (layer,) + tuple(index_map(*ids)))


def norm_proj(x, g, w, out_dtype, tn, tm=PROJ_TM, layer=None):
    t, d = x.shape
    n = w.shape[-1]
    return pl.pallas_call(
        _norm_proj_kernel,
        grid=(t // tm, n // tn),
        in_specs=[pl.BlockSpec((tm, d), lambda i, j: (i, 0)),
                  pl.BlockSpec((1, d), lambda i, j: (0, 0)),
                  _weight_spec(w, (d, tn), lambda i, j: (0, j), layer)],
        out_specs=pl.BlockSpec((tm, tn), lambda i, j: (i, j)),
        out_shape=jax.ShapeDtypeStruct((t, n), out_dtype),
        scratch_shapes=[pltpu.VMEM((tm, d), BF16)],
        compiler_params=pltpu.CompilerParams(dimension_semantics=("arbitrary", "arbitrary"),
                                             vmem_limit_bytes=VMEM_LIMIT),
        name="norm_proj",
    )(x, g.reshape(1, d), w)


def _out_proj_kernel(a_ref, w_ref, r_ref, o_ref):
    o_ref[...] = r_ref[...] + jnp.dot(a_ref[...], w_ref[...], preferred_element_type=F32)


def _out_proj_norm_kernel(a_ref, w_ref, r_ref, g_ref, o_ref):
    y = r_ref[...] + jnp.dot(a_ref[...], w_ref[...], preferred_element_type=F32)
    ms = jnp.mean(y * y, axis=-1, keepdims=True)
    o_ref[...] = y * lax.rsqrt(ms + EPS) * g_ref[...]


def out_proj(a, w, resid, final_g=None, layer=None):
    t, k = a.shape
    n = w.shape[-1]
    in_specs = [pl.BlockSpec((PROJ_TM, k), lambda i: (i, 0)),
                _weight_spec(w, (k, n), lambda i: (0, 0), layer),
                pl.BlockSpec((PROJ_TM, n), lambda i: (i, 0))]
    args = [a, w, resid]
    body = _out_proj_kernel
    if final_g is not None:
        in_specs.append(pl.BlockSpec((1, n), lambda i: (0, 0)))
        args.append(final_g.reshape(1, n))
        body = _out_proj_norm_kernel
    return pl.pallas_call(
        body,
        grid=(t // PROJ_TM,),
        in_specs=in_specs,
        out_specs=pl.BlockSpec((PROJ_TM, n), lambda i: (i, 0)),
        out_shape=jax.ShapeDtypeStruct((t, n), F32),
        compiler_params=pltpu.CompilerParams(dimension_semantics=("arbitrary",),
                                             vmem_limit_bytes=VMEM_LIMIT),
        name="out_proj",
    )(*args)


def _gdn_in_proj_kernel(x_ref, g_ref, w_ref, cw_ref, wab_ref, alog_ref, dtb_ref,
                        o_ref, bg_ref, gct_ref, pre_ref, hist_ref, *, tiles_per_seq):
    i = pl.program_id(0)
    tm = x_ref.shape[0]
    tn = GDN_PROJ_TN
    dk = GDN_DK
    dot = functools.partial(jnp.dot, preferred_element_type=F32)

    x = x_ref[...]
    ms = jnp.mean(x * x, axis=-1, keepdims=True)
    xn = x * lax.rsqrt(ms + EPS) * g_ref[...]
    xh = xn.astype(BF16)

    def project(c):
        return dot(xh, w_ref[:, c * tn:(c + 1) * tn])

    first = (i % tiles_per_seq) == 0

    def stage(c, pre):
        pre_ref[c, 0:CONV_TAIL, :] = jnp.where(first, 0.0, hist_ref[c])
        pre_ref[c, CONV_TAIL:, :] = pre
        hist_ref[c] = pre[tm - CONV_TAIL:, :]

    def finish(c, normalize):
        xe = pre_ref[c]
        cw = cw_ref[:, c * tn:(c + 1) * tn]
        acc = xe[CONV_TAIL:, :] * cw[CONV_K - 1:CONV_K, :]
        for tap in reversed(range(CONV_K - 1)):
            shifted = pltpu.roll(xe, CONV_K - 1 - tap, axis=0)
            acc = acc + shifted[CONV_TAIL:, :] * cw[tap:tap + 1, :]
        y = _silu(acc)
        if normalize:
            groups = []
            for h in range(tn // dk):
                yh = y[:, h * dk:(h + 1) * dk]
                inv = lax.rsqrt(jnp.sum(yh * yh, axis=-1, keepdims=True) + EPS)
                if h < GDN_HEADS:
                    inv = inv * (dk ** -0.5)
                groups.append(yh * inv)
            y = jnp.concatenate(groups, axis=1)
        o_ref[:, c * tn:(c + 1) * tn] = y

    stage(0, project(0))

    xl = (xn - xh.astype(F32)).astype(BF16)
    w = wab_ref[...]
    wh = w.astype(BF16)
    wl = (w - wh.astype(F32)).astype(BF16)
    ab = dot(xh, wh) + (dot(xl, wh) + dot(xh, wl))
    g = -jnp.exp(alog_ref[...]) * _softplus(ab + dtb_ref[...])
    beta = jax.nn.sigmoid(ab)
    row = lax.broadcasted_iota(jnp.int32, (tm, tm), 0)
    col = lax.broadcasted_iota(jnp.int32, (tm, tm), 1)
    tri = jnp.where((row >= col) & ((row ^ col) < GDN_CHUNK), 1.0, 0.0).astype(BF16)
    hi, mid, lo = _split3(g)
    gc = dot(tri, hi) + (dot(tri, mid) + dot(tri, lo))
    lane = lax.broadcasted_iota(jnp.int32, (tm, LANES), 1)
    bg_ref[...] = jnp.where(lane < GDN_HEADS, gc, beta)
    gct_ref[...] = gc.T[0:GDN_HEADS, :]

    pre = project(1)
    finish(0, True)
    stage(1, pre)
    pre = project(2)
    finish(1, False)
    o_ref[:, 2 * tn:] = _silu(pre)


def gdn_in_proj(x, g, w, conv_w, layer, w_ab, a_log, dt_bias, seq):
    t, d = x.shape
    tm, tn = GDN_BLOCK, GDN_PROJ_TN
    n = 3 * tn
    assert n == GDN_MAIN_W and conv_w.shape[-1] == 2 * tn and seq % tm == 0
    lane_row = lambda v: jnp.pad(v.astype(F32), (0, LANES - v.shape[0])).reshape(1, LANES)
    return pl.pallas_call(
        functools.partial(_gdn_in_proj_kernel, tiles_per_seq=seq // tm),
        grid=(t // tm,),
        in_specs=[pl.BlockSpec((tm, d), lambda i: (i, 0)),
                  pl.BlockSpec((1, d), lambda i: (0, 0)),
                  _weight_spec(w, (d, n), lambda i: (0, 0), layer),
                  _weight_spec(conv_w, (CONV_K, 2 * tn), lambda i: (0, 0), layer),
                  pl.BlockSpec((d, LANES), lambda i: (0, 0)),
                  pl.BlockSpec((1, LANES), lambda i: (0, 0)),
                  pl.BlockSpec((1, LANES), lambda i: (0, 0))],
        out_specs=[pl.BlockSpec((tm, n), lambda i: (i, 0)),
                   pl.BlockSpec((tm, LANES), lambda i: (i, 0)),
                   pl.BlockSpec((None, GDN_HEADS, GDN_BLOCK), lambda i: (i, 0, 0))],
        out_shape=[jax.ShapeDtypeStruct((t, n), F32),
                   jax.ShapeDtypeStruct((t, LANES), F32),
                   jax.ShapeDtypeStruct((t // GDN_BLOCK, GDN_HEADS, GDN_BLOCK), F32)],
        scratch_shapes=[pltpu.VMEM((2, tm + CONV_TAIL, tn), F32),
                        pltpu.VMEM((2, CONV_TAIL, tn), F32)],
        compiler_params=pltpu.CompilerParams(dimension_semantics=("arbitrary",),
                                             vmem_limit_bytes=VMEM_LIMIT),
        name="gdn_in_proj",
    )(x, g.reshape(1, d), w, conv_w, w_ab, lane_row(a_log), lane_row(dt_bias))


def _unit_lower_inverse_minus_eye(mats, r):
    n = mats[0].shape[0]
    d = [jnp.where(r < INV_BASE, a, 0.0) for a in mats]
    d2 = [_mm(x, x) for x in d]
    d3 = [_mm(x, y) for x, y in zip(d, d2)]
    d4 = [_mm(y, y) for y in d2]
    e = [y - x - z for x, y, z in zip(d, d2, d3)]
    e = [x + y + _mm(x, y) for x, y in zip(e, d4)]
    b = INV_BASE
    while b < n:
        m = [jnp.where((r >= b) & (r < 2 * b), a, 0.0) for a in mats]
        x = [mi + _mm(ei, mi) for ei, mi in zip(e, m)]
        e = [ei - xi - _mm(xi, ei) for ei, xi in zip(e, x)]
        b *= 2
    return e


def _gdn_core_kernel(q_ref, k_ref, v_ref, gate_ref, bg_ref, gct_ref, og_ref, wo_ref, r_ref, x_ref, s_ref):
    hp = GDN_HEADS
    tb, c = GDN_BLOCK, GDN_CHUNK
    dk, dv = GDN_DK, GDN_DV

    @pl.when(pl.program_id(1) == 0)
    def _():
        s_ref[...] = jnp.zeros_like(s_ref)

    bg = bg_ref[...]
    lane = lax.broadcasted_iota(jnp.int32, bg.shape, 1)
    row = lax.broadcasted_iota(jnp.int32, (c, c), 0)
    col = lax.broadcasted_iota(jnp.int32, (c, c), 1)
    incl = row >= col
    strict = row > col
    r = row ^ col

    units = [(hh, ci) for ci in range(tb // c) for hh in range(hp)]
    q, k, v, gc_col, beta, gc_row = [], [], [], [], [], []
    for hh in range(hp):
        q.append(q_ref[:, hh * dk:(hh + 1) * dk])
        k.append(k_ref[:, hh * dk:(hh + 1) * dk])
        v.append(v_ref[:, hh * dv:(hh + 1) * dv])
        gc_col.append(jnp.sum(jnp.where(lane == hh, bg, 0.0), axis=1, keepdims=True))
        beta.append(jnp.sum(jnp.where(lane == hh + GDN_HEADS, bg, 0.0), axis=1, keepdims=True))
        gc_row.append(gct_ref[hh:hh + 1, :])
    pick = lambda arrs: [arrs[hh][ci * c:(ci + 1) * c] for hh, ci in units]
    q, k, v, gc_col, beta = pick(q), pick(k), pick(v), pick(gc_col), pick(beta)
    gc_row = [gc_row[hh][:, ci * c:(ci + 1) * c] for hh, ci in units]
    gc_last = [g[:, c - 1:c] for g in gc_row]

    k_t = [x.T for x in k]
    decay = [jnp.where(incl, jnp.exp(jnp.where(incl, gcc - gcr, 0.0)), 0.0)
             for gcc, gcr in zip(gc_col, gc_row)]
    kb = [ki * bi for ki, bi in zip(k, beta)]
    a = [jnp.where(strict, _mm(kbi, kti) * di, 0.0) for kbi, kti, di in zip(kb, k_t, decay)]
    qk = [_mm(qi, kti) * di for qi, kti, di in zip(q, k_t, decay)]
    egc = [jnp.exp(g) for g in gc_col]
    rhs = [jnp.concatenate([kbi * ei, vi * bi], axis=1) for kbi, ei, vi, bi in zip(kb, egc, v, beta)]
    e = _unit_lower_inverse_minus_eye(a, r)
    sol = [x + _mm(ei, x) for ei, x in zip(e, rhs)]
    q_dec = [qi * ei for qi, ei in zip(q, egc)]
    k_dec_t = [kti * jnp.exp(gl - gcr) for kti, gl, gcr in zip(k_t, gc_last, gc_row)]
    both = [_mm(jnp.concatenate([kdt, qki], axis=0), x) for kdt, qki, x in zip(k_dec_t, qk, sol)]
    s_coef = [jnp.concatenate([x[0:dk, 0:dk], qd - x[dk:, 0:dk]], axis=0) for x, qd in zip(both, q_dec)]
    s_add = [x[0:dk, dk:] for x in both]
    o_add = [x[dk:, dk:] for x in both]

    state = [s_ref[hh] for hh in range(hp)]
    outs = [[] for _ in range(hp)]
    for u, (hh, ci) in enumerate(units):
        ps = _mm(s_coef[u], state[hh])
        outs[hh].append(o_add[u] + ps[dk:])
        state[hh] = state[hh] * jnp.exp(gc_last[u]) + (s_add[u] - ps[0:dk])
    gated = []
    for hh in range(hp):
        s_ref[hh] = state[hh]
        o = jnp.concatenate(outs[hh], axis=0)
        on = o * lax.rsqrt(jnp.mean(o * o, axis=-1, keepdims=True) + EPS) * og_ref[...]
        gated.append((on * gate_ref[:, hh * dv:(hh + 1) * dv]).astype(BF16))
    x_ref[...] = r_ref[...] + jnp.dot(jnp.concatenate(gated, axis=1), wo_ref[...],
                                      preferred_element_type=F32)


def gdn_core(proj, bg, gct, out_g, w_out, layer, resid, batch):
    t, d = resid.shape
    tb = GDN_BLOCK
    nt = t // batch // tb
    dk, dv, nh = GDN_DK, GDN_DV, GDN_HEADS
    wk, wv = nh * dk, nh * dv
    rowblk = lambda b, i: b * nt + i
    return pl.pallas_call(
        _gdn_core_kernel,
        grid=(batch, nt),
        in_specs=[pl.BlockSpec((tb, wk), lambda b, i: (rowblk(b, i), 0)),
                  pl.BlockSpec((tb, wk), lambda b, i: (rowblk(b, i), 1)),
                  pl.BlockSpec((tb, wv), lambda b, i: (rowblk(b, i), 1)),
                  pl.BlockSpec((tb, wv), lambda b, i: (rowblk(b, i), 2)),
                  pl.BlockSpec((tb, LANES), lambda b, i: (rowblk(b, i), 0)),
                  pl.BlockSpec((None, nh, tb), lambda b, i: (rowblk(b, i), 0, 0)),
                  pl.BlockSpec((1, dv), lambda b, i: (0, 0)),
                  _weight_spec(w_out, (wv, d), lambda b, i: (0, 0), layer),
                  pl.BlockSpec((tb, d), lambda b, i: (rowblk(b, i), 0))],
        out_specs=pl.BlockSpec((tb, d), lambda b, i: (rowblk(b, i), 0)),
        out_shape=jax.ShapeDtypeStruct((t, d), F32),
        scratch_shapes=[pltpu.VMEM((nh, dk, dv), F32)],
        compiler_params=pltpu.CompilerParams(
            dimension_semantics=("arbitrary", "arbitrary"),
            vmem_limit_bytes=VMEM_LIMIT),
        name="gdn_core",
    )(proj, proj, proj, proj, bg, gct, out_g.reshape(1, dv), w_out, resid)


def gdn_layer(x, batch, layer, norm_g, w_in, w_in_bf16, conv_w, a_log, dt_bias, out_g, w_out_bf16):
    w_ab = jnp.pad(w_in[layer][:, GDN_MAIN_W:], ((0, 0), (0, LANES - 2 * GDN_HEADS)))
    proj, bg, gct = gdn_in_proj(x, norm_g, w_in_bf16, conv_w, layer, w_ab, a_log, dt_bias, x.shape[0] // batch)
    return gdn_core(proj, bg, gct, out_g, w_out_bf16, layer, x, batch)


def _sb_core_kernel(q_ref, k_ref, v_ref, z_ref, o_ref, acc_ref, later_ref):
    blk, nqb = SB_BLOCK, SB_QBLOCKS
    i0 = pl.program_id(2) * nqb
    row = lax.broadcasted_iota(jnp.int32, (blk, blk), 0)
    col = lax.broadcasted_iota(jnp.int32, (blk, blk), 1)
    suffix = jnp.where(row >= col, 1.0, 0.0).astype(BF16)
    dot = functools.partial(jnp.dot, preferred_element_type=F32)
    rows = [slice(c * blk, (c + 1) * blk) for c in range(nqb)]
    q = [(q_ref[r, :] * (SB_DH ** -0.5)).astype(BF16) for r in rows]

    def scores(qc, j):
        start = pl.multiple_of(j * blk, blk)
        z = _mm_nt(qc, k_ref[pl.ds(start, blk), :])
        return z, -(jnp.maximum(z, 0.0) + jnp.log(1.0 + jnp.exp(-jnp.abs(z))))

    def suffix_sums(log1m):
        hi, mid, lo = _split3(log1m)
        return dot(hi, suffix) + (dot(mid, suffix) + dot(lo, suffix))

    def weighted_values(j, attn):
        start = pl.multiple_of(j * blk, blk)
        return dot(attn.astype(BF16), v_ref[pl.ds(start, blk), :])

    idx = [i0 + c for c in range(nqb)]
    has_prev = [ix > 0 for ix in idx]
    jp = [jnp.maximum(ix - 1, 0) for ix in idx]
    valid = col < row
    zl_d = [scores(qc, ix) for qc, ix in zip(q, idx)]
    zl_p = [scores(qc, j) for qc, j in zip(q, jp)]
    rc_d = [suffix_sums(jnp.where(valid, l, 0.0)) for _, l in zl_d]
    rc_p = [suffix_sums(jnp.where(hp, l, 0.0)) for (_, l), hp in zip(zl_p, has_prev)]
    later_d = [x[:, 0:1] for x in rc_d]
    attn_d = [jnp.where(valid, jnp.exp(z + rc), 0.0) for (z, _), rc in zip(zl_d, rc_d)]
    attn_p = [jnp.where(hp, jnp.exp(z + (rc + ld)), 0.0)
              for (z, _), rc, ld, hp in zip(zl_p, rc_p, later_d, has_prev)]
    for c, r in enumerate(rows):
        acc_ref[r, :] = weighted_values(idx[c], attn_d[c]) + weighted_values(jp[c], attn_p[c])
        later_ref[r, :] = later_d[c] + rc_p[c][:, 0:1]

    for c, r in enumerate(rows):
        def cond(carry):
            j, worst = carry
            return jnp.logical_and(j >= 0, worst > -SB_SKIP)

        def body(carry, c=c, r=r):
            j, _ = carry
            z, log1m = scores(q[c], j)
            rc_in = suffix_sums(log1m)
            later = later_ref[r, :]
            acc_ref[r, :] += weighted_values(j, jnp.exp(z + (rc_in + later)))
            later_ref[r, :] = later + rc_in[:, 0:1]
            return j - 1, jnp.max(later_ref[r, :])

        lax.while_loop(cond, body, (idx[c] - 2, jnp.max(later_ref[r, :])))
    o_ref[...] = (acc_ref[...] * _silu(z_ref[...])).astype(o_ref.dtype)


def sb_core(proj, kv, batch):
    t = proj.shape[0]
    s = t // batch
    qrows = SB_BLOCK * SB_QBLOCKS
    nq = s // qrows
    nh, dh = SB_HEADS, SB_DH
    return pl.pallas_call(
        _sb_core_kernel,
        grid=(batch, nh, nq),
        in_specs=[pl.BlockSpec((qrows, dh), lambda b, h, i: (b * nq + i, h)),
                  pl.BlockSpec((s, dh), lambda b, h, i: (b, h)),
                  pl.BlockSpec((s, dh), lambda b, h, i: (b, nh + h)),
                  pl.BlockSpec((qrows, dh), lambda b, h, i: (b * nq + i, nh + h))],
        out_specs=pl.BlockSpec((qrows, dh), lambda b, h, i: (b * nq + i, h)),
        out_shape=jax.ShapeDtypeStruct((t, nh * dh), BF16),
        scratch_shapes=[pltpu.VMEM((qrows, dh), F32),
                        pltpu.VMEM((qrows, 1), F32)],
        compiler_params=pltpu.CompilerParams(
            dimension_semantics=("arbitrary", "arbitrary", "arbitrary"),
            vmem_limit_bytes=VMEM_LIMIT),
        name="sb_core",
    )(proj, kv, kv, proj)


def sb_layer(x, batch, layer, norm_g, w_in_bf16, w_out_bf16, kv, final_g):
    proj = norm_proj(x, norm_g, w_in_bf16, F32, tn=2 * SB_W, layer=layer)
    o = sb_core(proj, kv, batch)
    return out_proj(o, w_out_bf16, x, final_g, layer=layer)


def kernel(x, a_norm, a_w_in, a_conv, a_A_log, a_dt_bias, a_out_norm, a_w_out, kv_norm, w_kv, b_norm,
           b_w_in, b_w_out, final_norm):
    batch, seq, d = x.shape
    h = x.reshape(batch * seq, d)
    a_w_in_bf16, a_w_out_bf16 = a_w_in.astype(BF16), a_w_out.astype(BF16)
    b_w_in_bf16, b_w_out_bf16 = b_w_in.astype(BF16), b_w_out.astype(BF16)
    for l in range(a_w_in.shape[0]):
        h = gdn_layer(h, batch, l, a_norm[l], a_w_in, a_w_in_bf16, a_conv, a_A_log[l], a_dt_bias[l],
                      a_out_norm[l], a_w_out_bf16)
    kv = norm_proj(h, kv_norm, w_kv.astype(BF16), BF16, tn=2 * SB_W)
    n_b = b_w_in.shape[0]
    for j in range(n_b):
        h = sb_layer(h, batch, j, b_norm[j], b_w_in_bf16, b_w_out_bf16, kv,
                     final_norm if j == n_b - 1 else None)
    return h.reshape(batch, seq, d)
```
